```python
import math
import jax
import jax.numpy as jnp
from jax import lax
import numpy as np

D_MODEL = 2048
BATCH = 1
SEQ = 8192
DEPTH = 1

N_META = 16
ML_HEADS = 8
ML_HEAD_DIM = D_MODEL // ML_HEADS
ML_WIDTH = ML_HEADS * ML_HEAD_DIM
ML_CHUNK = 64
ML_FGATE_BIAS_LO = 3.0
ML_FGATE_BIAS_HI = 6.0
HY_WIDTH = D_MODEL
HY_ORDER = 2
HY_SHORT_K = 3
HY_BANDS = 16
HY_EMB_DIM = 1 + 2 * HY_BANDS
HY_FILT_HIDDEN = 64
HY_FAST_DECAY_PCT = 0.3
HY_SLOW_DECAY_PCT = 1.5
HY_DECAY_TARGET = 1e-2
N_EXPERTS = 16
EC_CAPACITY = 2
D_EXPERT = 5504
N_BRANCHES = 2
RMS_EPS = 1e-6
NEG_GATE = -1e9

OFF_Q = 0
OFF_K = OFF_Q + ML_WIDTH
OFF_V = OFF_K + ML_WIDTH
OFF_O = OFF_V + ML_WIDTH
OFF_G = OFF_O + ML_WIDTH
OFF_HY = OFF_G + 4 * ML_HEADS
OFF_MERGE = OFF_HY + 3 * HY_WIDTH
P_IN = OFF_MERGE + N_BRANCHES * D_MODEL

kernel_name = 'hybrid_mlstm_hyena_ec_moe_encoder'


def rmsnorm(x, g):
    xf = x.astype(jnp.float32)
    y = xf * lax.rsqrt(jnp.mean(xf * xf, axis=-1, keepdims=True) + RMS_EPS)
    return (y * g.astype(jnp.float32)).astype(x.dtype)


def mlstm_chunkwise(q, k, v, log_i, log_f):
    b_, nh, t_len, dh = q.shape
    nc = t_len // ML_CHUNK
    qc = q.reshape(b_, nh, nc, ML_CHUNK, dh)
    kc = k.reshape(b_, nh, nc, ML_CHUNK, dh)
    vc = v.reshape(b_, nh, nc, ML_CHUNK, dh)
    li = log_i.reshape(b_, nh, nc, ML_CHUNK)
    cum_f = jnp.cumsum(log_f.reshape(b_, nh, nc, ML_CHUNK), axis=-1)
    tot_f = cum_f[..., -1]
    a = tot_f[..., None] - cum_f + li

    def step(carry, xs):
        c_st, n_st, m_st = carry
        k_j, v_j, a_j, g_j = xs
        m_new = jnp.maximum(g_j + m_st, jnp.max(a_j, axis=-1))
        decay = jnp.exp(g_j + m_st - m_new)
        kw = k_j * jnp.exp(a_j - m_new[..., None])[..., None]
        c_new = decay[..., None, None] * c_st + jnp.einsum('bhsk,bhsv->bhkv', kw, v_j)
        n_new = decay[..., None] * n_st + jnp.sum(kw, axis=2)
        return (c_new, n_new, m_new), (c_st, n_st, m_st)

    init = (jnp.zeros((b_, nh, dh, dh), jnp.float32),
            jnp.zeros((b_, nh, dh), jnp.float32),
            jnp.zeros((b_, nh), jnp.float32))
    xs = (jnp.moveaxis(kc, 2, 0), jnp.moveaxis(vc, 2, 0), jnp.moveaxis(a, 2, 0), jnp.moveaxis(tot_f, 2, 0))
    _, (c_prev, n_prev, m_prev) = lax.scan(step, init, xs)
    c_prev = jnp.moveaxis(c_prev, 0, 2)
    n_prev = jnp.moveaxis(n_prev, 0, 2)
    m_prev = jnp.moveaxis(m_prev, 0, 2)

    causal = jnp.tril(jnp.ones((ML_CHUNK, ML_CHUNK), dtype=bool))
    log_d = cum_f[..., :, None] - cum_f[..., None, :] + li[..., None, :]
    log_d = jnp.where(causal, log_d, -jnp.inf)
    log_inter = cum_f + m_prev[..., None]
    m_t = jnp.maximum(log_inter, jnp.max(log_d, axis=-1))
    s = jnp.einsum('bhctk,bhcsk->bhcts', qc, kc) * jnp.exp(log_d - m_t[..., None])
    w_inter = jnp.exp(log_inter - m_t)
    num = (jnp.einsum('bhcts,bhcsv->bhctv', s, vc)
           + w_inter[..., None] * jnp.einsum('bhctk,bhckv->bhctv', qc, c_prev))
    den = jnp.sum(s, axis=-1) + w_inter * jnp.einsum('bhctk,bhck->bhct', qc, n_prev)
    h = num / jnp.maximum(jnp.abs(den), jnp.exp(-m_t))[..., None]
    return h.reshape(b_, nh, t_len, dh)


def mlstm_branch(q, k, v, o_pre, gate_pre, gate_bias, head_norm):
    f32 = jnp.float32
    b_, seq_len, _ = q.shape
    pad = ML_CHUNK - N_META

    def to_heads(t):
        t = t.astype(f32).reshape(b_, seq_len, ML_HEADS, ML_HEAD_DIM).transpose(0, 2, 1, 3)
        return jnp.pad(t, ((0, 0), (0, 0), (pad, 0), (0, 0)))

    qh = to_heads(q)
    kh = to_heads(k) * (ML_HEAD_DIM ** -0.5)
    vh = to_heads(v)
    gates = (gate_pre.astype(f32) + gate_bias.astype(f32)).reshape(b_, seq_len, 4, ML_HEADS).transpose(2, 0, 3, 1)

    def pad_gate(t, fill):
        return jnp.pad(t, ((0, 0), (0, 0), (pad, 0)), constant_values=fill)

    li_fw = pad_gate(gates[0], NEG_GATE)
    lf_fw = pad_gate(jax.nn.log_sigmoid(gates[1]), 0.0)
    li_bw = pad_gate(gates[2], NEG_GATE)
    lf_bw = pad_gate(jax.nn.log_sigmoid(gates[3]), 0.0)
    h_fw = mlstm_chunkwise(qh, kh, vh, li_fw, lf_fw)

    def flip(t):
        return jnp.flip(t, axis=2)

    h_bw = flip(mlstm_chunkwise(flip(qh), flip(kh), flip(vh), flip(li_bw), flip(lf_bw)))
    h = (h_fw + h_bw)[:, :, pad:]
    hn = h * lax.rsqrt(jnp.mean(h * h, axis=-1, keepdims=True) + RMS_EPS)
    hn = hn * head_norm.astype(f32).reshape(ML_HEADS, 1, ML_HEAD_DIM)
    hn = hn.transpose(0, 2, 1, 3).reshape(b_, seq_len, ML_WIDTH)
    return (jax.nn.sigmoid(o_pre.astype(f32)) * hn).astype(q.dtype)


def centred_short_conv(u, w):
    r = w.shape[0] // 2
    return lax.conv_general_dilated(u, w[:, None, :].astype(u.dtype), window_strides=(1,),
                                    padding=[(r, r)], dimension_numbers=('NWC', 'WIO', 'NWC'),
                                    feature_group_count=u.shape[-1])


def hyena_filters(seq_len, w1, b1, w2, b2, freq, w3):
    f32 = jnp.float32
    pos = jnp.arange(seq_len, dtype=f32)[:, None]
    t = pos / (seq_len - 1)
    bands = jnp.linspace(1e-4, HY_BANDS - 1, HY_BANDS, dtype=f32)[None, :]
    ang = bands * (2.0 * math.pi) * pos / seq_len
    emb = jnp.concatenate([t, jnp.cos(ang), -jnp.sin(ang)], axis=-1)
    fr = freq.astype(f32)
    z = jnp.sin(fr * (emb @ w1.astype(f32) + b1.astype(f32)))
    z = jnp.sin(fr * (z @ w2.astype(f32) + b2.astype(f32)))
    filt = (z @ w3.astype(f32)).reshape(seq_len, 2 * HY_ORDER, HY_WIDTH)
    max_decay = math.log(HY_DECAY_TARGET) / HY_FAST_DECAY_PCT
    min_decay = math.log(HY_DECAY_TARGET) / HY_SLOW_DECAY_PCT
    deltas = jnp.abs(jnp.linspace(min_decay, max_decay, HY_WIDTH, dtype=f32))
    window = jnp.exp(-t * deltas[None, :])
    return filt * window[:, None, :]


def bidirectional_long_conv(z, h_fw, h_bw):
    seq_len = z.shape[1]
    kern = jnp.concatenate([h_fw, jnp.zeros((1, h_fw.shape[1]), h_fw.dtype), h_bw[1:][::-1]], axis=0)
    zf = jnp.fft.rfft(z, n=2 * seq_len, axis=1)
    kf = jnp.fft.rfft(kern, axis=0)
    return jnp.fft.irfft(zf * kf[None], n=2 * seq_len, axis=1)[:, :seq_len]


def hyena_branch(u, conv_w, filters, bias):
    f32 = jnp.float32
    uc = centred_short_conv(u, conv_w).astype(f32)
    v, x1, x2 = jnp.split(uc, 3, axis=-1)
    z = v
    for order, gate in enumerate((x1, x2)):
        conv = bidirectional_long_conv(z, filters[:, 2 * order], filters[:, 2 * order + 1])
        z = gate * (conv + bias[order].astype(f32) * z)
    return z.astype(u.dtype)


def expert_choice_ffn(u, w_router, w_gate, w_up, w_down):
    b_, seq_len, d = u.shape
    cap = EC_CAPACITY * seq_len // N_EXPERTS
    logits = jnp.einsum('bld,de->ble', u, w_router).astype(jnp.float32)
    aff = jax.nn.softmax(logits, axis=-1)
    gate, idx = lax.top_k(jnp.swapaxes(aff, 1, 2), cap)
    xe = jax.vmap(lambda ub, ib: ub[ib])(u, idx)
    hid = jax.nn.silu(jnp.einsum('becd,edf->becf', xe, w_gate)) * jnp.einsum('becd,edf->becf', xe, w_up)
    ye = jnp.einsum('becf,efd->becd', hid, w_down) * gate[..., None].astype(u.dtype)
    return jax.vmap(lambda yb, ib: jnp.zeros((seq_len, d), yb.dtype).at[ib.reshape(-1)].add(yb.reshape(-1, d)))(ye, idx)


def setup_inputs(seed: int = 0) -> dict:
    key = jax.random.key(seed)
    ks = jax.random.split(key, 24)

    def nrm(k, shape, scale):
        return scale * jax.random.normal(k, shape, jnp.float32)

    fgate = jnp.linspace(ML_FGATE_BIAS_LO, ML_FGATE_BIAS_HI, ML_HEADS, dtype=jnp.float32)
    zeros_h = jnp.zeros((ML_HEADS,), jnp.float32)
    gate_base = jnp.concatenate([zeros_h, fgate, zeros_h, fgate])
    return {
        'x': nrm(ks[0], (BATCH, SEQ, D_MODEL), 1.0),
        'meta_tokens': nrm(ks[1], (N_META, D_MODEL), 1.0),
        'norm_mix': 1.0 + nrm(ks[2], (DEPTH, D_MODEL), 0.01),
        'w_in': nrm(ks[3], (DEPTH, D_MODEL, P_IN), D_MODEL ** -0.5),
        'ml_gate_bias': gate_base[None, :] + nrm(ks[4], (DEPTH, 4 * ML_HEADS), 0.1),
        'ml_head_norm': 1.0 + nrm(ks[5], (DEPTH, ML_WIDTH), 0.01),
        'hy_conv_w': nrm(ks[6], (DEPTH, HY_SHORT_K, 3 * HY_WIDTH), HY_SHORT_K ** -0.5),
        'hy_filt_w1': nrm(ks[7], (DEPTH, HY_EMB_DIM, HY_FILT_HIDDEN), HY_EMB_DIM ** -0.5),
        'hy_filt_b1': nrm(ks[8], (DEPTH, HY_FILT_HIDDEN), 0.1),
        'hy_filt_w2': nrm(ks[9], (DEPTH, HY_FILT_HIDDEN, HY_FILT_HIDDEN), HY_FILT_HIDDEN ** -0.5),
        'hy_filt_b2': nrm(ks[10], (DEPTH, HY_FILT_HIDDEN), 0.1),
        'hy_filt_freq': 1.0 + nrm(ks[11], (DEPTH, HY_FILT_HIDDEN), 0.05),
        'hy_filt_w3': nrm(ks[12], (DEPTH, HY_FILT_HIDDEN, 2 * HY_ORDER * HY_WIDTH), 0.005),
        'hy_bias': nrm(ks[13], (DEPTH, HY_ORDER, HY_WIDTH), 0.5),
        'w_branch_a': nrm(ks[14], (DEPTH, ML_WIDTH, D_MODEL), ML_WIDTH ** -0.5),
        'w_branch_b': nrm(ks[15], (DEPTH, HY_WIDTH, D_MODEL), HY_WIDTH ** -0.5),
        'w_out': nrm(ks[16], (DEPTH, D_MODEL, D_MODEL), D_MODEL ** -0.5),
        'norm_ffn': 1.0 + nrm(ks[17], (DEPTH, D_MODEL), 0.01),
        'w_router': nrm(ks[18], (DEPTH, D_MODEL, N_EXPERTS), D_MODEL ** -0.5),
        'w_gate': nrm(ks[19], (DEPTH, N_EXPERTS, D_MODEL, D_EXPERT), D_MODEL ** -0.5),
        'w_up': nrm(ks[20], (DEPTH, N_EXPERTS, D_MODEL, D_EXPERT), D_MODEL ** -0.5),
        'w_down': nrm(ks[21], (DEPTH, N_EXPERTS, D_EXPERT, D_MODEL), D_EXPERT ** -0.5),
        'norm_final': 1.0 + nrm(ks[22], (D_MODEL,), 0.01),
    }


def reference(x, meta_tokens, norm_mix, w_in, ml_gate_bias, ml_head_norm, hy_conv_w,
              hy_filt_w1, hy_filt_b1, hy_filt_w2, hy_filt_b2, hy_filt_freq, hy_filt_w3, hy_bias,
              w_branch_a, w_branch_b, w_out, norm_ffn, w_router, w_gate, w_up, w_down, norm_final):
    b_ = x.shape[0]
    meta = jnp.broadcast_to(meta_tokens[None].astype(x.dtype), (b_, N_META, D_MODEL))
    h = jnp.concatenate([meta, x], axis=1)
    seq_len = h.shape[1]
    for layer in range(DEPTH):
        u = rmsnorm(h, norm_mix[layer])
        p = jnp.einsum('bld,dp->blp', u, w_in[layer])
        y_a = mlstm_branch(p[..., OFF_Q:OFF_K], p[..., OFF_K:OFF_V], p[..., OFF_V:OFF_O],
                           p[..., OFF_O:OFF_G], p[..., OFF_G:OFF_HY],
                           ml_gate_bias[layer], ml_head_norm[layer])
        filters = hyena_filters(seq_len, hy_filt_w1[layer], hy_filt_b1[layer], hy_filt_w2[layer],
                                hy_filt_b2[layer], hy_filt_freq[layer], hy_filt_w3[layer])
        y_b = hyena_branch(p[..., OFF_HY:OFF_MERGE], hy_conv_w[layer], filters, hy_bias[layer])
        g_a = jax.nn.sigmoid(p[..., OFF_MERGE:OFF_MERGE + D_MODEL])
        g_b = jax.nn.sigmoid(p[..., OFF_MERGE + D_MODEL:P_IN])
        merged = (g_a * jnp.einsum('blw,wd->bld', y_a, w_branch_a[layer])
                  + g_b * jnp.einsum('blw,wd->bld', y_b, w_branch_b[layer]))
        h = h + jnp.einsum('bld,de->ble', merged, w_out[layer])
        h = h + expert_choice_ffn(rmsnorm(h, norm_ffn[layer]), w_router[layer],
                                  w_gate[layer], w_up[layer], w_down[layer])
    return rmsnorm(h, norm_final)[:, N_META:]
```

```python
import functools
import math

import jax
import jax.numpy as jnp
from jax import lax
from jax.experimental import pallas as pl
from jax.experimental.pallas import tpu as pltpu

D_MODEL = 2048
N_META = 16
ML_HEADS = 8
ML_HEAD_DIM = D_MODEL // ML_HEADS
ML_WIDTH = ML_HEADS * ML_HEAD_DIM
ML_CHUNK = 64
HY_WIDTH = D_MODEL
HY_ORDER = 2
HY_BANDS = 16
HY_FAST_DECAY_PCT = 0.3
HY_SLOW_DECAY_PCT = 1.5
HY_DECAY_TARGET = 1e-2
N_EXPERTS = 16
EC_CAPACITY = 2
D_EXPERT = 5504
RMS_EPS = 1e-6
NEG_GATE = -1e9

OFF_G = 4 * ML_WIDTH
OFF_HY = OFF_G + 4 * ML_HEADS
OFF_MERGE = OFF_HY + 3 * HY_WIDTH

VMEM_LIMIT_BYTES = 56 * 1024 * 1024

F32 = jnp.float32
BF16 = jnp.bfloat16


def _cparams(sem):
    return pltpu.CompilerParams(dimension_semantics=sem, vmem_limit_bytes=VMEM_LIMIT_BYTES)


def _norm_proj_kernel(x_ref, g_ref, w_ref, o_ref, u_ref):
    @pl.when(pl.program_id(1) == 0)
    def _():
        x = x_ref[...]
        y = x * lax.rsqrt(jnp.mean(x * x, axis=-1, keepdims=True) + RMS_EPS)
        u_ref[...] = (y * g_ref[...]).astype(BF16)

    o_ref[...] = jnp.dot(u_ref[...], w_ref[...].astype(BF16), preferred_element_type=F32)


def norm_proj(x, g, w, n_cols, col_block0, tm, tn):
    m, d = x.shape
    grid = (m // tm, n_cols // tn)
    return pl.pallas_call(
        _norm_proj_kernel,
        grid=grid,
        in_specs=[
            pl.BlockSpec((tm, d), lambda i, j: (i, 0)),
            pl.BlockSpec((1, d), lambda i, j: (0, 0)),
            pl.BlockSpec((d, tn), lambda i, j: (0, j + col_block0)),
        ],
        out_specs=pl.BlockSpec((tm, tn), lambda i, j: (i, j)),
        out_shape=jax.ShapeDtypeStruct((m, n_cols), F32),
        scratch_shapes=[pltpu.VMEM((tm, d), BF16)],
        compiler_params=_cparams(("parallel", "arbitrary")),
        name="norm_proj",
    )(x, g.reshape(1, d), w)


def _gated_dual_mm_kernel(ya_ref, yb_ref, pa_ref, pb_ref, wa_ref, wb_ref, o_ref):
    za = jnp.dot(ya_ref[...], wa_ref[...].astype(BF16), preferred_element_type=F32)
    zb = jnp.dot(yb_ref[...], wb_ref[...].astype(BF16), preferred_element_type=F32)
    merged = jax.nn.sigmoid(pa_ref[...]) * za + jax.nn.sigmoid(pb_ref[...]) * zb
    o_ref[...] = merged.astype(o_ref.dtype)


def gated_dual_mm(ya, yb, p_merge, wa, wb, tm, tn):
    m, d = ya.shape
    nb = d // tn
    return pl.pallas_call(
        _gated_dual_mm_kernel,
        grid=(m // tm, nb),
        in_specs=[pl.BlockSpec((tm, d), lambda i, j: (i, 0)),
                  pl.BlockSpec((tm, d), lambda i, j: (i, 0)),
                  pl.BlockSpec((tm, tn), lambda i, j: (i, j)),
                  pl.BlockSpec((tm, tn), lambda i, j: (i, j + nb)),
                  pl.BlockSpec((d, tn), lambda i, j: (0, j)),
                  pl.BlockSpec((d, tn), lambda i, j: (0, j))],
        out_specs=pl.BlockSpec((tm, tn), lambda i, j: (i, j)),
        out_shape=jax.ShapeDtypeStruct((m, d), BF16),
        compiler_params=_cparams(("parallel", "arbitrary")),
        name="gated_dual_mm",
    )(ya, yb, p_merge, p_merge, wa, wb)


def _mm_res_kernel(a_ref, b_ref, r_ref, o_ref):
    o_ref[...] = r_ref[...] + jnp.dot(a_ref[...], b_ref[...].astype(BF16), preferred_element_type=F32)


def mm_residual(a, b, res, tm, tn):
    m, k = a.shape
    _, n = b.shape
    return pl.pallas_call(
        _mm_res_kernel,
        grid=(m // tm, n // tn),
        in_specs=[pl.BlockSpec((tm, k), lambda i, j: (i, 0)),
                  pl.BlockSpec((k, tn), lambda i, j: (0, j)),
                  pl.BlockSpec((tm, tn), lambda i, j: (i, j))],
        out_specs=pl.BlockSpec((tm, tn), lambda i, j: (i, j)),
        out_shape=jax.ShapeDtypeStruct((m, n), F32),
        compiler_params=_cparams(("parallel", "arbitrary")),
        name="mm_residual",
    )(a, b, res)


def _ffn_kernel(xe_ref, gt_ref, wg_ref, wu_ref, wd_ref, o_ref, *, tf, f_total):
    f = pl.program_id(1)
    x = xe_ref[0]
    a = jnp.dot(x, wg_ref[0].astype(BF16), preferred_element_type=F32)
    b = jnp.dot(x, wu_ref[0].astype(BF16), preferred_element_type=F32)
    hid = (a * jax.nn.sigmoid(a)) * b
    col_ok = (f * tf + lax.broadcasted_iota(jnp.int32, (1, tf), 1)) < f_total
    row_ok = (f * tf + lax.broadcasted_iota(jnp.int32, (tf, 1), 0)) < f_total
    hid = jnp.where(col_ok, hid, 0.0).astype(BF16)
    wd = jnp.where(row_ok, wd_ref[0], 0.0).astype(BF16)
    y = jnp.dot(hid, wd, preferred_element_type=F32)

    @pl.when(f == 0)
    def _():
        o_ref[0] = y

    @pl.when(f > 0)
    def _():
        o_ref[0] += y

    @pl.when(f == pl.num_programs(1) - 1)
    def _():
        o_ref[0] = o_ref[0] * gt_ref[0]


def expert_ffn(xe, gate, wg, wu, wd, tf):
    e, c, d = xe.shape
    f_total = wg.shape[-1]
    nf = pl.cdiv(f_total, tf)
    return pl.pallas_call(
        functools.partial(_ffn_kernel, tf=tf, f_total=f_total),
        grid=(e, nf),
        in_specs=[pl.BlockSpec((1, c, d), lambda i, f: (i, 0, 0)),
                  pl.BlockSpec((1, c, 1), lambda i, f: (i, 0, 0)),
                  pl.BlockSpec((1, d, tf), lambda i, f: (i, 0, f)),
                  pl.BlockSpec((1, d, tf), lambda i, f: (i, 0, f)),
                  pl.BlockSpec((1, tf, d), lambda i, f: (i, f, 0))],
        out_specs=pl.BlockSpec((1, c, d), lambda i, f: (i, 0, 0)),
        out_shape=jax.ShapeDtypeStruct((e, c, d), F32),
        compiler_params=_cparams(("parallel", "arbitrary")),
        name="expert_ffn",
    )(xe, gate, wg, wu, wd)


def _rmsnorm_kernel(x_ref, g_ref, o_ref):
    x = x_ref[...]
    o_ref[...] = x * lax.rsqrt(jnp.mean(x * x, axis=-1, keepdims=True) + RMS_EPS) * g_ref[...]


def rmsnorm_rows(x, g, tm):
    m, d = x.shape
    return pl.pallas_call(
        _rmsnorm_kernel,
        grid=(m // tm,),
        in_specs=[pl.BlockSpec((tm, d), lambda i: (i, 0)), pl.BlockSpec((1, d), lambda i: (0, 0))],
        out_specs=pl.BlockSpec((tm, d), lambda i: (i, 0)),
        out_shape=jax.ShapeDtypeStruct((m, d), F32),
        compiler_params=_cparams(("parallel",)),
        name="rmsnorm",
    )(x, g.reshape(1, d))


def _rmsnorm_jnp(x, g):
    return x * lax.rsqrt(jnp.mean(x * x, axis=-1, keepdims=True) + RMS_EPS) * g


def _mlstm_chunkwise(q, k, v, log_i, log_f):
    b_, nh, t_len, dh = q.shape
    nc = t_len // ML_CHUNK
    qc = q.reshape(b_, nh, nc, ML_CHUNK, dh)
    kc = k.reshape(b_, nh, nc, ML_CHUNK, dh)
    vc = v.reshape(b_, nh, nc, ML_CHUNK, dh)
    li = log_i.reshape(b_, nh, nc, ML_CHUNK)
    cum_f = jnp.cumsum(log_f.reshape(b_, nh, nc, ML_CHUNK), axis=-1)
    tot_f = cum_f[..., -1]
    a = tot_f[..., None] - cum_f + li

    def step(carry, xs):
        c_st, n_st, m_st = carry
        k_j, v_j, a_j, g_j = xs
        m_new = jnp.maximum(g_j + m_st, jnp.max(a_j, axis=-1))
        decay = jnp.exp(g_j + m_st - m_new)
        kw = k_j * jnp.exp(a_j - m_new[..., None])[..., None]
        c_new = decay[..., None, None] * c_st + jnp.einsum('bhsk,bhsv->bhkv', kw, v_j)
        n_new = decay[..., None] * n_st + jnp.sum(kw, axis=2)
        return (c_new, n_new, m_new), (c_st, n_st, m_st)

    init = (jnp.zeros((b_, nh, dh, dh), F32), jnp.zeros((b_, nh, dh), F32), jnp.zeros((b_, nh), F32))
    xs = (jnp.moveaxis(kc, 2, 0), jnp.moveaxis(vc, 2, 0), jnp.moveaxis(a, 2, 0), jnp.moveaxis(tot_f, 2, 0))
    _, (c_prev, n_prev, m_prev) = lax.scan(step, init, xs)
    c_prev = jnp.moveaxis(c_prev, 0, 2)
    n_prev = jnp.moveaxis(n_prev, 0, 2)
    m_prev = jnp.moveaxis(m_prev, 0, 2)

    causal = jnp.tril(jnp.ones((ML_CHUNK, ML_CHUNK), dtype=bool))
    log_d = cum_f[..., :, None] - cum_f[..., None, :] + li[..., None, :]
    log_d = jnp.where(causal, log_d, -jnp.inf)
    log_inter = cum_f + m_prev[..., None]
    m_t = jnp.maximum(log_inter, jnp.max(log_d, axis=-1))
    s = jnp.einsum('bhctk,bhcsk->bhcts', qc, kc) * jnp.exp(log_d - m_t[..., None])
    w_inter = jnp.exp(log_inter - m_t)
    num = (jnp.einsum('bhcts,bhcsv->bhctv', s, vc)
           + w_inter[..., None] * jnp.einsum('bhctk,bhckv->bhctv', qc, c_prev))
    den = jnp.sum(s, axis=-1) + w_inter * jnp.einsum('bhctk,bhck->bhct', qc, n_prev)
    h = num / jnp.maximum(jnp.abs(den), jnp.exp(-m_t))[..., None]
    return h.reshape(b_, nh, t_len, dh)


def _mlstm_branch(q, k, v, o_pre, gate_pre, gate_bias, head_norm):
    b_, seq_len, _ = q.shape
    pad = ML_CHUNK - N_META

    def to_heads(t):
        t = t.reshape(b_, seq_len, ML_HEADS, ML_HEAD_DIM).transpose(0, 2, 1, 3)
        return jnp.pad(t, ((0, 0), (0, 0), (pad, 0), (0, 0)))

    qh = to_heads(q)
    kh = to_heads(k) * (ML_HEAD_DIM ** -0.5)
    vh = to_heads(v)
    gates = (gate_pre + gate_bias).reshape(b_, seq_len, 4, ML_HEADS).transpose(2, 0, 3, 1)

    def pad_gate(t, fill):
        return jnp.pad(t, ((0, 0), (0, 0), (pad, 0)), constant_values=fill)

    li_fw = pad_gate(gates[0], NEG_GATE)
    lf_fw = pad_gate(jax.nn.log_sigmoid(gates[1]), 0.0)
    li_bw = pad_gate(gates[2], NEG_GATE)
    lf_bw = pad_gate(jax.nn.log_sigmoid(gates[3]), 0.0)
    h_fw = _mlstm_chunkwise(qh, kh, vh, li_fw, lf_fw)

    def flip(t):
        return jnp.flip(t, axis=2)

    h_bw = flip(_mlstm_chunkwise(flip(qh), flip(kh), flip(vh), flip(li_bw), flip(lf_bw)))
    h = (h_fw + h_bw)[:, :, pad:]
    hn = h * lax.rsqrt(jnp.mean(h * h, axis=-1, keepdims=True) + RMS_EPS)
    hn = hn * head_norm.reshape(ML_HEADS, 1, ML_HEAD_DIM)
    hn = hn.transpose(0, 2, 1, 3).reshape(b_, seq_len, ML_WIDTH)
    return jax.nn.sigmoid(o_pre) * hn


def _short_conv(u, w):
    r = w.shape[0] // 2
    return lax.conv_general_dilated(u, w[:, None, :], window_strides=(1,), padding=[(r, r)],
                                    dimension_numbers=('NWC', 'WIO', 'NWC'),
                                    feature_group_count=u.shape[-1])


def _hyena_filters(seq_len, w1, b1, w2, b2, freq, w3):
    pos = jnp.arange(seq_len, dtype=F32)[:, None]
    t = pos / (seq_len - 1)
    bands = jnp.linspace(1e-4, HY_BANDS - 1, HY_BANDS, dtype=F32)[None, :]
    ang = bands * (2.0 * math.pi) * pos / seq_len
    emb = jnp.concatenate([t, jnp.cos(ang), -jnp.sin(ang)], axis=-1)
    z = jnp.sin(freq * (emb @ w1 + b1))
    z = jnp.sin(freq * (z @ w2 + b2))
    filt = (z @ w3).reshape(seq_len, 2 * HY_ORDER, HY_WIDTH)
    max_decay = math.log(HY_DECAY_TARGET) / HY_FAST_DECAY_PCT
    min_decay = math.log(HY_DECAY_TARGET) / HY_SLOW_DECAY_PCT
    deltas = jnp.abs(jnp.linspace(min_decay, max_decay, HY_WIDTH, dtype=F32))
    window = jnp.exp(-t * deltas[None, :])
    return filt * window[:, None, :]


def _long_conv(z, h_fw, h_bw):
    seq_len = z.shape[1]
    kern = jnp.concatenate([h_fw, jnp.zeros((1, h_fw.shape[1]), h_fw.dtype), h_bw[1:][::-1]], axis=0)
    zf = jnp.fft.rfft(z, n=2 * seq_len, axis=1)
    kf = jnp.fft.rfft(kern, axis=0)
    return jnp.fft.irfft(zf * kf[None], n=2 * seq_len, axis=1)[:, :seq_len]


def _hyena_branch(u, conv_w, filters, bias):
    uc = _short_conv(u, conv_w)
    v, x1, x2 = jnp.split(uc, 3, axis=-1)
    z = v
    for order, gate in enumerate((x1, x2)):
        conv = _long_conv(z, filters[:, 2 * order], filters[:, 2 * order + 1])
        z = gate * (conv + bias[order] * z)
    return z


def kernel(x, meta_tokens, norm_mix, w_in, ml_gate_bias, ml_head_norm, hy_conv_w, hy_filt_w1, hy_filt_b1,
           hy_filt_w2, hy_filt_b2, hy_filt_freq, hy_filt_w3, hy_bias, w_branch_a, w_branch_b, w_out,
           norm_ffn, w_router, w_gate, w_up, w_down, norm_final):
    b_ = x.shape[0]
    assert b_ == 1
    h = jnp.concatenate([meta_tokens, x[0]], axis=0)
    seq_len = h.shape[0]
    tm = 912
    assert seq_len % tm == 0
    layer = 0

    w_in_l = w_in[layer]
    qkvo = norm_proj(h, norm_mix[layer], w_in_l, OFF_G, 0, tm, 1024)
    w_hm = w_in_l[:, OFF_HY:]
    p_hm = norm_proj(h, norm_mix[layer], w_hm, w_hm.shape[1], 0, tm, 1024)
    w_g = jnp.pad(w_in_l[:, OFF_G:OFF_HY], ((0, 0), (0, 128 - 4 * ML_HEADS)))
    p_g = norm_proj(h, norm_mix[layer], w_g, 128, 0, tm, 128)[:, :4 * ML_HEADS]

    y_a = _mlstm_branch(qkvo[None, :, 0:ML_WIDTH], qkvo[None, :, ML_WIDTH:2 * ML_WIDTH],
                        qkvo[None, :, 2 * ML_WIDTH:3 * ML_WIDTH], qkvo[None, :, 3 * ML_WIDTH:],
                        p_g[None], ml_gate_bias[layer], ml_head_norm[layer])[0]
    filters = _hyena_filters(seq_len, hy_filt_w1[layer], hy_filt_b1[layer], hy_filt_w2[layer],
                             hy_filt_b2[layer], hy_filt_freq[layer], hy_filt_w3[layer])
    y_b = _hyena_branch(p_hm[None, :, :3 * HY_WIDTH], hy_conv_w[layer], filters, hy_bias[layer])[0]

    merged = gated_dual_mm(y_a.astype(BF16), y_b.astype(BF16), p_hm[:, 3 * HY_WIDTH:],
                           w_branch_a[layer], w_branch_b[layer], tm, 512)
    h = mm_residual(merged, w_out[layer], h, tm, 1024)

    u = _rmsnorm_jnp(h, norm_ffn[layer])
    cap = EC_CAPACITY * seq_len // N_EXPERTS
    logits = jnp.dot(u, w_router[layer])
    aff = jax.nn.softmax(logits, axis=-1)
    gate, idx = lax.top_k(aff.T, cap)
    cap_pad = 1040
    idx_p = jnp.pad(idx, ((0, 0), (0, cap_pad - cap)))
    gate_p = jnp.pad(gate, ((0, 0), (0, cap_pad - cap)))
    xe = u.astype(BF16)[idx_p]
    ye = expert_ffn(xe, gate_p[..., None], w_gate[layer], w_up[layer], w_down[layer], 256)
    h = h.at[idx_p.reshape(-1)].add(ye.reshape(-1, D_MODEL))

    out = rmsnorm_rows(h, norm_final, tm)
    return out[None, N_META:]
```

```python
import functools
import math

import numpy as np
import jax
import jax.numpy as jnp
from jax import lax
from jax.experimental import pallas as pl
from jax.experimental.pallas import tpu as pltpu

D_MODEL = 2048
N_META = 16
ML_HEADS = 8
ML_HEAD_DIM = D_MODEL // ML_HEADS
ML_WIDTH = ML_HEADS * ML_HEAD_DIM
ML_CHUNK = 64
HY_WIDTH = D_MODEL
HY_ORDER = 2
HY_BANDS = 16
HY_FAST_DECAY_PCT = 0.3
HY_SLOW_DECAY_PCT = 1.5
HY_DECAY_TARGET = 1e-2
N_EXPERTS = 16
EC_CAPACITY = 2
D_EXPERT = 5504
RMS_EPS = 1e-6
NEG_GATE = -1e9

OFF_G = 4 * ML_WIDTH
OFF_HY = OFF_G + 4 * ML_HEADS
OFF_MERGE = OFF_HY + 3 * HY_WIDTH

VMEM_LIMIT_BYTES = 56 * 1024 * 1024

F32 = jnp.float32
BF16 = jnp.bfloat16


def _cparams(sem):
    return pltpu.CompilerParams(dimension_semantics=sem, vmem_limit_bytes=VMEM_LIMIT_BYTES)


def _norm_proj_kernel(x_ref, g_ref, w_ref, o_ref, u_ref):
    @pl.when(pl.program_id(1) == 0)
    def _():
        x = x_ref[...]
        y = x * lax.rsqrt(jnp.mean(x * x, axis=-1, keepdims=True) + RMS_EPS)
        u_ref[...] = (y * g_ref[...]).astype(BF16)

    o_ref[...] = jnp.dot(u_ref[...], w_ref[...].astype(BF16), preferred_element_type=F32)


def norm_proj(x, g, w, n_cols, col_block0, tm, tn):
    m, d = x.shape
    grid = (m // tm, n_cols // tn)
    return pl.pallas_call(
        _norm_proj_kernel,
        grid=grid,
        in_specs=[
            pl.BlockSpec((tm, d), lambda i, j: (i, 0)),
            pl.BlockSpec((1, d), lambda i, j: (0, 0)),
            pl.BlockSpec((d, tn), lambda i, j: (0, j + col_block0)),
        ],
        out_specs=pl.BlockSpec((tm, tn), lambda i, j: (i, j)),
        out_shape=jax.ShapeDtypeStruct((m, n_cols), F32),
        scratch_shapes=[pltpu.VMEM((tm, d), BF16)],
        compiler_params=_cparams(("parallel", "arbitrary")),
        name="norm_proj",
    )(x, g.reshape(1, d), w)


def _gated_dual_mm_kernel(ya_ref, yb_ref, pa_ref, pb_ref, wa_ref, wb_ref, o_ref):
    za = jnp.dot(ya_ref[...], wa_ref[...].astype(BF16), preferred_element_type=F32)
    zb = jnp.dot(yb_ref[...], wb_ref[...].astype(BF16), preferred_element_type=F32)
    merged = jax.nn.sigmoid(pa_ref[...]) * za + jax.nn.sigmoid(pb_ref[...]) * zb
    o_ref[...] = merged.astype(o_ref.dtype)


def gated_dual_mm(ya, yb, p_merge, gate_col0, wa, wb, tm, tn):
    m, d = ya.shape
    nb = d // tn
    g0 = gate_col0 // tn
    return pl.pallas_call(
        _gated_dual_mm_kernel,
        grid=(m // tm, nb),
        in_specs=[pl.BlockSpec((tm, d), lambda i, j: (i, 0)),
                  pl.BlockSpec((tm, d), lambda i, j: (i, 0)),
                  pl.BlockSpec((tm, tn), lambda i, j: (i, j + g0)),
                  pl.BlockSpec((tm, tn), lambda i, j: (i, j + g0 + nb)),
                  pl.BlockSpec((d, tn), lambda i, j: (0, j)),
                  pl.BlockSpec((d, tn), lambda i, j: (0, j))],
        out_specs=pl.BlockSpec((tm, tn), lambda i, j: (i, j)),
        out_shape=jax.ShapeDtypeStruct((m, d), BF16),
        compiler_params=_cparams(("parallel", "arbitrary")),
        name="gated_dual_mm",
    )(ya, yb, p_merge, p_merge, wa, wb)


def _mm_res_kernel(a_ref, b_ref, r_ref, o_ref):
    o_ref[...] = r_ref[...] + jnp.dot(a_ref[...], b_ref[...].astype(BF16), preferred_element_type=F32)


def mm_residual(a, b, res, tm, tn):
    m, k = a.shape
    _, n = b.shape
    return pl.pallas_call(
        _mm_res_kernel,
        grid=(m // tm, n // tn),
        in_specs=[pl.BlockSpec((tm, k), lambda i, j: (i, 0)),
                  pl.BlockSpec((k, tn), lambda i, j: (0, j)),
                  pl.BlockSpec((tm, tn), lambda i, j: (i, j))],
        out_specs=pl.BlockSpec((tm, tn), lambda i, j: (i, j)),
        out_shape=jax.ShapeDtypeStruct((m, n), F32),
        compiler_params=_cparams(("parallel", "arbitrary")),
        name="mm_residual",
    )(a, b, res)


def _ffn_kernel(xe_ref, gt_ref, wg_ref, wu_ref, wd_ref, o_ref, *, tf, f_total):
    f = pl.program_id(1)
    x = xe_ref[0]
    a = jnp.dot(x, wg_ref[0].astype(BF16), preferred_element_type=F32)
    b = jnp.dot(x, wu_ref[0].astype(BF16), preferred_element_type=F32)
    hid = (a * jax.nn.sigmoid(a)) * b
    col_ok = (f * tf + lax.broadcasted_iota(jnp.int32, (1, tf), 1)) < f_total
    row_ok = (f * tf + lax.broadcasted_iota(jnp.int32, (tf, 1), 0)) < f_total
    hid = jnp.where(col_ok, hid, 0.0).astype(BF16)
    wd = jnp.where(row_ok, wd_ref[0], 0.0).astype(BF16)
    y = jnp.dot(hid, wd, preferred_element_type=F32)

    @pl.when(f == 0)
    def _():
        o_ref[0] = y

    @pl.when(f > 0)
    def _():
        o_ref[0] += y

    @pl.when(f == pl.num_programs(1) - 1)
    def _():
        o_ref[0] = o_ref[0] * gt_ref[0]


def expert_ffn(xe, gate, wg, wu, wd, tf):
    e, c, d = xe.shape
    f_total = wg.shape[-1]
    nf = pl.cdiv(f_total, tf)
    return pl.pallas_call(
        functools.partial(_ffn_kernel, tf=tf, f_total=f_total),
        grid=(e, nf),
        in_specs=[pl.BlockSpec((1, c, d), lambda i, f: (i, 0, 0)),
                  pl.BlockSpec((1, c, 1), lambda i, f: (i, 0, 0)),
                  pl.BlockSpec((1, d, tf), lambda i, f: (i, 0, f)),
                  pl.BlockSpec((1, d, tf), lambda i, f: (i, 0, f)),
                  pl.BlockSpec((1, tf, d), lambda i, f: (i, f, 0))],
        out_specs=pl.BlockSpec((1, c, d), lambda i, f: (i, 0, 0)),
        out_shape=jax.ShapeDtypeStruct((e, c, d), F32),
        compiler_params=_cparams(("parallel", "arbitrary")),
        name="expert_ffn",
    )(xe, gate, wg, wu, wd)


def _rmsnorm_kernel(x_ref, g_ref, o_ref):
    x = x_ref[...]
    o_ref[...] = x * lax.rsqrt(jnp.mean(x * x, axis=-1, keepdims=True) + RMS_EPS) * g_ref[...]


def rmsnorm_rows(x, g, tm):
    m, d = x.shape
    return pl.pallas_call(
        _rmsnorm_kernel,
        grid=(m // tm,),
        in_specs=[pl.BlockSpec((tm, d), lambda i: (i, 0)), pl.BlockSpec((1, d), lambda i: (0, 0))],
        out_specs=pl.BlockSpec((tm, d), lambda i: (i, 0)),
        out_shape=jax.ShapeDtypeStruct((m, d), F32),
        compiler_params=_cparams(("parallel",)),
        name="rmsnorm",
    )(x, g.reshape(1, d))


FFT_N1 = 256
FFT_N2 = 72
FFT_N = FFT_N1 * FFT_N2
SEQ_SLABS = 128
SEQ_PAD = SEQ_SLABS * FFT_N2
LANES = 128
K1_CHUNK = 32
HY_ORDERS = 2
FILT_ROWS = 1152
HIGHEST = lax.Precision.HIGHEST


def fft_tables():
    n1 = np.arange(FFT_N1)
    ang1 = 2.0 * np.pi * np.outer(n1, n1) / FFT_N1
    c1, s1 = np.cos(ang1), np.sin(ang1)
    f1_fwd = np.concatenate([c1, -s1], axis=0)
    f1_inv = np.concatenate([c1, -s1], axis=1) / FFT_N
    k2 = np.arange(FFT_N2)
    n2 = np.arange(FFT_N2)
    k1 = np.arange(FFT_N1)
    ang = 2.0 * np.pi * (n2[None, None, :] * (k1[:, None, None] + FFT_N1 * k2[None, :, None])) / FFT_N
    g = np.concatenate([np.cos(ang), -np.sin(ang)], axis=2)
    angt = np.transpose(ang, (0, 2, 1))
    h = np.concatenate([np.cos(angt), -np.sin(angt)], axis=2)
    return (jnp.asarray(f1_fwd, F32), jnp.asarray(f1_inv, F32), jnp.asarray(g, F32), jnp.asarray(h, F32))


def _stage1_fwd(x_ref, f1_ref, buf_ref, n_rows):
    f1 = f1_ref[:, 0:n_rows].astype(BF16)

    def body(j, carry):
        na = 2 * j
        sa = x_ref[pl.ds(na, n_rows, stride=FFT_N2), :]
        sb = x_ref[pl.ds(na + 1, n_rows, stride=FFT_N2), :]
        s = jnp.concatenate([sa, sb], axis=1).astype(BF16)
        u = jnp.dot(f1, s, preferred_element_type=F32)
        buf_ref[0, pl.ds(na, FFT_N1, stride=FFT_N2), :] = u[0:FFT_N1, 0:LANES]
        buf_ref[0, pl.ds(na + 1, FFT_N1, stride=FFT_N2), :] = u[0:FFT_N1, LANES:]
        buf_ref[1, pl.ds(na, FFT_N1, stride=FFT_N2), :] = u[FFT_N1:, 0:LANES]
        buf_ref[1, pl.ds(na + 1, FFT_N1, stride=FFT_N2), :] = u[FFT_N1:, LANES:]
        return carry

    lax.fori_loop(0, FFT_N2 // 2, body, 0)


def _stage2_fwd(buf_ref, g_ref, row0, kk):
    ar = buf_ref[0, pl.ds(row0, FFT_N2), :]
    ai = buf_ref[1, pl.ds(row0, FFT_N2), :]
    t = jnp.concatenate([jnp.concatenate([ar, ai], axis=1),
                         jnp.concatenate([-ai, ar], axis=1)], axis=0).astype(BF16)
    x = jnp.dot(g_ref[kk].astype(BF16), t, preferred_element_type=F32)
    return x[:, 0:LANES], x[:, LANES:]


def _spectrum_kernel(k_ref, f1_ref, g_ref, o_ref, buf_ref):
    c = pl.program_id(2)

    @pl.when(c == 0)
    def _():
        _stage1_fwd(k_ref, f1_ref, buf_ref, FFT_N1)

    def body(kk, carry):
        row0 = pl.multiple_of((c * K1_CHUNK + kk) * FFT_N2, 8)
        xr, xi = _stage2_fwd(buf_ref, g_ref, row0, kk)
        o0 = pl.multiple_of(kk * FFT_N2, 8)
        o_ref[0, pl.ds(o0, FFT_N2), :] = xr
        o_ref[1, pl.ds(o0, FFT_N2), :] = xi
        return carry

    lax.fori_loop(0, K1_CHUNK, body, 0)


def filter_spectrum(kern, f1_fwd, g):
    no, n, ch = kern.shape
    nchunk = FFT_N1 // K1_CHUNK
    return pl.pallas_call(
        _spectrum_kernel,
        grid=(no, ch // LANES, nchunk),
        in_specs=[pl.BlockSpec((None, n, LANES), lambda o, b, c: (o, 0, b)),
                  pl.BlockSpec((2 * FFT_N1, FFT_N1), lambda o, b, c: (0, 0)),
                  pl.BlockSpec((K1_CHUNK, FFT_N2, 2 * FFT_N2), lambda o, b, c: (c, 0, 0))],
        out_specs=pl.BlockSpec((None, 2, K1_CHUNK * FFT_N2, LANES), lambda o, b, c: (o, 0, c, b)),
        out_shape=jax.ShapeDtypeStruct((no, 2, n, ch), F32),
        scratch_shapes=[pltpu.VMEM((2, FFT_N, LANES), F32)],
        compiler_params=_cparams(("parallel", "parallel", "arbitrary")),
        name="filter_spectrum",
    )(kern, f1_fwd, g)


def _conv_kernel(z_ref, kf_ref, f1_ref, f1i_ref, g_ref, h_ref, o_ref, buf_ref):
    c = pl.program_id(1)

    @pl.when(c == 0)
    def _():
        _stage1_fwd(z_ref, f1_ref, buf_ref, SEQ_SLABS)

    def body(kk, carry):
        row0 = pl.multiple_of((c * K1_CHUNK + kk) * FFT_N2, 8)
        xr, xi = _stage2_fwd(buf_ref, g_ref, row0, kk)
        k0 = pl.multiple_of(kk * FFT_N2, 8)
        kr = kf_ref[0, pl.ds(k0, FFT_N2), :]
        ki = kf_ref[1, pl.ds(k0, FFT_N2), :]
        pr = xr * kr - xi * ki
        pi = xr * ki + xi * kr
        t = jnp.concatenate([jnp.concatenate([pr, pi], axis=1),
                             jnp.concatenate([pi, -pr], axis=1)], axis=0).astype(BF16)
        q = jnp.dot(h_ref[kk].astype(BF16), t, preferred_element_type=F32)
        buf_ref[0, pl.ds(row0, FFT_N2), :] = q[:, 0:LANES]
        buf_ref[1, pl.ds(row0, FFT_N2), :] = q[:, LANES:]
        return carry

    lax.fori_loop(0, K1_CHUNK, body, 0)

    @pl.when(c == pl.num_programs(1) - 1)
    def _():
        f1i = f1i_ref[0:SEQ_SLABS, :].astype(BF16)

        def inv_body(j, carry):
            na = 2 * j
            sra = buf_ref[0, pl.ds(na, FFT_N1, stride=FFT_N2), :]
            srb = buf_ref[0, pl.ds(na + 1, FFT_N1, stride=FFT_N2), :]
            sia = buf_ref[1, pl.ds(na, FFT_N1, stride=FFT_N2), :]
            sib = buf_ref[1, pl.ds(na + 1, FFT_N1, stride=FFT_N2), :]
            s = jnp.concatenate([jnp.concatenate([sra, srb], axis=1),
                                 jnp.concatenate([sia, sib], axis=1)], axis=0).astype(BF16)
            y = jnp.dot(f1i, s, preferred_element_type=F32)
            o_ref[pl.ds(na, SEQ_SLABS, stride=FFT_N2), :] = y[:, 0:LANES]
            o_ref[pl.ds(na + 1, SEQ_SLABS, stride=FFT_N2), :] = y[:, LANES:]
            return carry

        lax.fori_loop(0, FFT_N2 // 2, inv_body, 0)


def long_conv(z_src, z_block0, ch, kf, order, tables):
    f1_fwd, f1_inv, g, h = tables
    rows = z_src.shape[0]
    assert rows == SEQ_PAD
    nchunk = FFT_N1 // K1_CHUNK
    return pl.pallas_call(
        _conv_kernel,
        grid=(ch // LANES, nchunk),
        in_specs=[pl.BlockSpec((rows, LANES), lambda b, c: (0, b + z_block0)),
                  pl.BlockSpec((None, 2, K1_CHUNK * FFT_N2, LANES), lambda b, c: (order, 0, c, b)),
                  pl.BlockSpec((2 * FFT_N1, FFT_N1), lambda b, c: (0, 0)),
                  pl.BlockSpec((FFT_N1, 2 * FFT_N1), lambda b, c: (0, 0)),
                  pl.BlockSpec((K1_CHUNK, FFT_N2, 2 * FFT_N2), lambda b, c: (c, 0, 0)),
                  pl.BlockSpec((K1_CHUNK, FFT_N2, 2 * FFT_N2), lambda b, c: (c, 0, 0))],
        out_specs=pl.BlockSpec((rows, LANES), lambda b, c: (0, b)),
        out_shape=jax.ShapeDtypeStruct((rows, ch), F32),
        scratch_shapes=[pltpu.VMEM((2, FFT_N, LANES), F32)],
        compiler_params=_cparams(("parallel", "arbitrary")),
        name="long_conv",
    )(z_src, kf, f1_fwd, f1_inv, g, h)


def _filter_kernel(emb_ref, w1_ref, b1_ref, w2_ref, b2_ref, fr_ref, w3_ref, dl_ref, o_ref, *, seq_len):
    i = pl.program_id(0)
    emb = emb_ref[...]
    fr = fr_ref[...]
    h1 = jnp.sin(fr * (jnp.dot(emb, w1_ref[...], precision=HIGHEST, preferred_element_type=F32) + b1_ref[...]))
    h2 = jnp.sin(fr * (jnp.dot(h1, w2_ref[...], precision=HIGHEST, preferred_element_type=F32) + b2_ref[...]))
    f = jnp.dot(h2, w3_ref[...], precision=HIGHEST, preferred_element_type=F32)
    t = emb[:, 0:1]
    f = f * jnp.exp(-t * dl_ref[...])
    r = i * FILT_ROWS + lax.broadcasted_iota(jnp.int32, (FILT_ROWS, 1), 0)
    fwd = i < (FFT_N // 2) // FILT_ROWS
    lo = jnp.where(fwd, -1, FFT_N - seq_len)
    hi = jnp.where(fwd, seq_len, FFT_N)
    valid = (r > lo) & (r < hi)
    o_ref[0] = jnp.where(valid, f, 0.0)


def filter_kernels(emb, w1p, b1, w2, b2, freq, w3, deltas, seq_len, tc=512):
    ch = deltas.shape[0]
    hid = w2.shape[0]
    ncb = ch // tc
    half = (FFT_N // 2) // FILT_ROWS
    full = lambda i, o, j: (0, 0)
    return pl.pallas_call(
        functools.partial(_filter_kernel, seq_len=seq_len),
        grid=(FFT_N // FILT_ROWS, HY_ORDERS, ncb),
        in_specs=[pl.BlockSpec((FILT_ROWS, emb.shape[1]), lambda i, o, j: (i, 0)),
                  pl.BlockSpec(w1p.shape, full), pl.BlockSpec((1, hid), full),
                  pl.BlockSpec((hid, hid), full), pl.BlockSpec((1, hid), full), pl.BlockSpec((1, hid), full),
                  pl.BlockSpec((hid, tc), lambda i, o, j: (0, (2 * o + i // half) * ncb + j)),
                  pl.BlockSpec((1, tc), lambda i, o, j: (0, j))],
        out_specs=pl.BlockSpec((1, FILT_ROWS, tc), lambda i, o, j: (o, i, j)),
        out_shape=jax.ShapeDtypeStruct((HY_ORDERS, FFT_N, ch), F32),
        compiler_params=_cparams(("parallel", "arbitrary", "arbitrary")),
        name="hyena_filters",
    )(emb, w1p, b1.reshape(1, hid), w2, b2.reshape(1, hid), freq.reshape(1, hid), w3, deltas.reshape(1, ch))


def _short_conv_kernel(u_ref, w_ref, o_ref):
    u = u_ref[...]
    rows = u.shape[0]
    r = lax.broadcasted_iota(jnp.int32, (rows, 1), 0)
    prev = jnp.where(r == 0, 0.0, pltpu.roll(u, 1, axis=0))
    nxt = jnp.where(r == rows - 1, 0.0, pltpu.roll(u, rows - 1, axis=0))
    w = w_ref[...]
    o_ref[0:rows, :] = w[0:1, :] * prev + w[1:2, :] * u + w[2:3, :] * nxt
    o_ref[rows:, :] = jnp.zeros((o_ref.shape[0] - rows, o_ref.shape[1]), F32)


def short_conv(p, conv_w, n_cols, col_block0=0):
    rows = p.shape[0]
    return pl.pallas_call(
        _short_conv_kernel,
        grid=(n_cols // LANES,),
        in_specs=[pl.BlockSpec((rows, LANES), lambda j: (0, j + col_block0)),
                  pl.BlockSpec((3, LANES), lambda j: (0, j))],
        out_specs=pl.BlockSpec((SEQ_PAD, LANES), lambda j: (0, j)),
        out_shape=jax.ShapeDtypeStruct((SEQ_PAD, n_cols), F32),
        compiler_params=_cparams(("parallel",)),
        name="short_conv",
    )(p, conv_w)


def _gate_kernel(g_ref, c_ref, z_ref, b_ref, o_ref):
    o_ref[...] = (g_ref[...] * (c_ref[...] + b_ref[...] * z_ref[...])).astype(o_ref.dtype)


def hyena_gate(uc, gate_block0, conv, z_src, z_block0, bias_row, rows, out_dtype, tr, tc=512):
    ch = conv.shape[1]
    return pl.pallas_call(
        _gate_kernel,
        grid=(rows // tr, ch // tc),
        in_specs=[pl.BlockSpec((tr, tc), lambda i, j: (i, j + gate_block0)),
                  pl.BlockSpec((tr, tc), lambda i, j: (i, j)),
                  pl.BlockSpec((tr, tc), lambda i, j: (i, j + z_block0)),
                  pl.BlockSpec((1, tc), lambda i, j: (0, j))],
        out_specs=pl.BlockSpec((tr, tc), lambda i, j: (i, j)),
        out_shape=jax.ShapeDtypeStruct((rows, ch), out_dtype),
        compiler_params=_cparams(("parallel", "parallel")),
        name="hyena_gate",
    )(uc, conv, z_src, bias_row)


def position_features(seq_len, n_bands):
    r = np.arange(FFT_N)
    pos = jnp.asarray(np.where(r < FFT_N // 2, r, FFT_N - r).astype(np.float32)[:, None])
    t = pos / (seq_len - 1)
    bands = jnp.linspace(1e-4, n_bands - 1, n_bands, dtype=F32)[None, :]
    ang = bands * (2.0 * math.pi) * pos / seq_len
    emb = jnp.concatenate([t, jnp.cos(ang), -jnp.sin(ang)], axis=-1)
    return jnp.pad(emb, ((0, 0), (0, LANES - emb.shape[1])))


def hyena_branch(p, col_block0, conv_w, w1, b1, w2, b2, freq, w3, bias, deltas, seq_len, n_bands, out_dtype):
    ch = deltas.shape[0]
    assert seq_len % FFT_N2 == 0 and seq_len // FFT_N2 <= SEQ_SLABS and 2 * seq_len - 1 <= FFT_N
    tables = fft_tables()
    emb = position_features(seq_len, n_bands)
    w1p = jnp.pad(w1, ((0, LANES - w1.shape[0]), (0, 0)))
    kern = filter_kernels(emb, w1p, b1, w2, b2, freq, w3, deltas, seq_len, tc=min(512, ch))
    kf = filter_spectrum(kern, tables[0], tables[2])
    uc = short_conv(p, conv_w, 3 * ch, col_block0)
    conv0 = long_conv(uc, 0, ch, kf, 0, tables)
    tcg = min(512, ch)
    gb = ch // tcg
    z1 = hyena_gate(uc, gb, conv0, uc, 0, bias[0:1], SEQ_PAD, F32, FILT_ROWS, tcg)
    conv1 = long_conv(z1, 0, ch, kf, 1, tables)
    return hyena_gate(uc, 2 * gb, conv1, z1, 0, bias[1:2], seq_len, out_dtype, 912, tcg)


def _rmsnorm_jnp(x, g):
    return x * lax.rsqrt(jnp.mean(x * x, axis=-1, keepdims=True) + RMS_EPS) * g


def _mlstm_chunkwise(q, k, v, log_i, log_f):
    b_, nh, t_len, dh = q.shape
    nc = t_len // ML_CHUNK
    qc = q.reshape(b_, nh, nc, ML_CHUNK, dh)
    kc = k.reshape(b_, nh, nc, ML_CHUNK, dh)
    vc = v.reshape(b_, nh, nc, ML_CHUNK, dh)
    li = log_i.reshape(b_, nh, nc, ML_CHUNK)
    cum_f = jnp.cumsum(log_f.reshape(b_, nh, nc, ML_CHUNK), axis=-1)
    tot_f = cum_f[..., -1]
    a = tot_f[..., None] - cum_f + li

    def step(carry, xs):
        c_st, n_st, m_st = carry
        k_j, v_j, a_j, g_j = xs
        m_new = jnp.maximum(g_j + m_st, jnp.max(a_j, axis=-1))
        decay = jnp.exp(g_j + m_st - m_new)
        kw = k_j * jnp.exp(a_j - m_new[..., None])[..., None]
        c_new = decay[..., None, None] * c_st + jnp.einsum('bhsk,bhsv->bhkv', kw, v_j)
        n_new = decay[..., None] * n_st + jnp.sum(kw, axis=2)
        return (c_new, n_new, m_new), (c_st, n_st, m_st)

    init = (jnp.zeros((b_, nh, dh, dh), F32), jnp.zeros((b_, nh, dh), F32), jnp.zeros((b_, nh), F32))
    xs = (jnp.moveaxis(kc, 2, 0), jnp.moveaxis(vc, 2, 0), jnp.moveaxis(a, 2, 0), jnp.moveaxis(tot_f, 2, 0))
    _, (c_prev, n_prev, m_prev) = lax.scan(step, init, xs)
    c_prev = jnp.moveaxis(c_prev, 0, 2)
    n_prev = jnp.moveaxis(n_prev, 0, 2)
    m_prev = jnp.moveaxis(m_prev, 0, 2)

    causal = jnp.tril(jnp.ones((ML_CHUNK, ML_CHUNK), dtype=bool))
    log_d = cum_f[..., :, None] - cum_f[..., None, :] + li[..., None, :]
    log_d = jnp.where(causal, log_d, -jnp.inf)
    log_inter = cum_f + m_prev[..., None]
    m_t = jnp.maximum(log_inter, jnp.max(log_d, axis=-1))
    s = jnp.einsum('bhctk,bhcsk->bhcts', qc, kc) * jnp.exp(log_d - m_t[..., None])
    w_inter = jnp.exp(log_inter - m_t)
    num = (jnp.einsum('bhcts,bhcsv->bhctv', s, vc)
           + w_inter[..., None] * jnp.einsum('bhctk,bhckv->bhctv', qc, c_prev))
    den = jnp.sum(s, axis=-1) + w_inter * jnp.einsum('bhctk,bhck->bhct', qc, n_prev)
    h = num / jnp.maximum(jnp.abs(den), jnp.exp(-m_t))[..., None]
    return h.reshape(b_, nh, t_len, dh)


def _mlstm_branch(q, k, v, o_pre, gate_pre, gate_bias, head_norm):
    b_, seq_len, _ = q.shape
    pad = ML_CHUNK - N_META

    def to_heads(t):
        t = t.reshape(b_, seq_len, ML_HEADS, ML_HEAD_DIM).transpose(0, 2, 1, 3)
        return jnp.pad(t, ((0, 0), (0, 0), (pad, 0), (0, 0)))

    qh = to_heads(q)
    kh = to_heads(k) * (ML_HEAD_DIM ** -0.5)
    vh = to_heads(v)
    gates = (gate_pre + gate_bias).reshape(b_, seq_len, 4, ML_HEADS).transpose(2, 0, 3, 1)

    def pad_gate(t, fill):
        return jnp.pad(t, ((0, 0), (0, 0), (pad, 0)), constant_values=fill)

    li_fw = pad_gate(gates[0], NEG_GATE)
    lf_fw = pad_gate(jax.nn.log_sigmoid(gates[1]), 0.0)
    li_bw = pad_gate(gates[2], NEG_GATE)
    lf_bw = pad_gate(jax.nn.log_sigmoid(gates[3]), 0.0)
    h_fw = _mlstm_chunkwise(qh, kh, vh, li_fw, lf_fw)

    def flip(t):
        return jnp.flip(t, axis=2)

    h_bw = flip(_mlstm_chunkwise(flip(qh), flip(kh), flip(vh), flip(li_bw), flip(lf_bw)))
    h = (h_fw + h_bw)[:, :, pad:]
    hn = h * lax.rsqrt(jnp.mean(h * h, axis=-1, keepdims=True) + RMS_EPS)
    hn = hn * head_norm.reshape(ML_HEADS, 1, ML_HEAD_DIM)
    hn = hn.transpose(0, 2, 1, 3).reshape(b_, seq_len, ML_WIDTH)
    return jax.nn.sigmoid(o_pre) * hn


def kernel(x, meta_tokens, norm_mix, w_in, ml_gate_bias, ml_head_norm, hy_conv_w, hy_filt_w1, hy_filt_b1,
           hy_filt_w2, hy_filt_b2, hy_filt_freq, hy_filt_w3, hy_bias, w_branch_a, w_branch_b, w_out,
           norm_ffn, w_router, w_gate, w_up, w_down, norm_final):
    b_ = x.shape[0]
    assert b_ == 1
    h = jnp.concatenate([meta_tokens, x[0]], axis=0)
    seq_len = h.shape[0]
    tm = 912
    assert seq_len % tm == 0
    layer = 0

    w_in_l = w_in[layer]
    qkvo = norm_proj(h, norm_mix[layer], w_in_l, OFF_G, 0, tm, 1024)
    w_hm = w_in_l[:, OFF_HY:]
    p_hm = norm_proj(h, norm_mix[layer], w_hm, w_hm.shape[1], 0, tm, 1024)
    w_g = jnp.pad(w_in_l[:, OFF_G:OFF_HY], ((0, 0), (0, 128 - 4 * ML_HEADS)))
    p_g = norm_proj(h, norm_mix[layer], w_g, 128, 0, tm, 128)[:, :4 * ML_HEADS]

    y_a = _mlstm_branch(qkvo[None, :, 0:ML_WIDTH], qkvo[None, :, ML_WIDTH:2 * ML_WIDTH],
                        qkvo[None, :, 2 * ML_WIDTH:3 * ML_WIDTH], qkvo[None, :, 3 * ML_WIDTH:],
                        p_g[None], ml_gate_bias[layer], ml_head_norm[layer])[0]
    max_decay = math.log(HY_DECAY_TARGET) / HY_FAST_DECAY_PCT
    min_decay = math.log(HY_DECAY_TARGET) / HY_SLOW_DECAY_PCT
    deltas = jnp.abs(jnp.linspace(min_decay, max_decay, HY_WIDTH, dtype=F32))
    y_b = hyena_branch(p_hm, 0, hy_conv_w[layer], hy_filt_w1[layer], hy_filt_b1[layer], hy_filt_w2[layer],
                       hy_filt_b2[layer], hy_filt_freq[layer], hy_filt_w3[layer], hy_bias[layer], deltas,
                       seq_len, HY_BANDS, BF16)

    merged = gated_dual_mm(y_a.astype(BF16), y_b, p_hm, 3 * HY_WIDTH,
                           w_branch_a[layer], w_branch_b[layer], tm, 512)
    h = mm_residual(merged, w_out[layer], h, tm, 1024)

    u = _rmsnorm_jnp(h, norm_ffn[layer])
    cap = EC_CAPACITY * seq_len // N_EXPERTS
    logits = jnp.dot(u, w_router[layer])
    aff = jax.nn.softmax(logits, axis=-1)
    gate, idx = lax.top_k(aff.T, cap)
    cap_pad = 1040
    idx_p = jnp.pad(idx, ((0, 0), (0, cap_pad - cap)))
    gate_p = jnp.pad(gate, ((0, 0), (0, cap_pad - cap)))
    xe = u.astype(BF16)[idx_p]
    ye = expert_ffn(xe, gate_p[..., None], w_gate[layer], w_up[layer], w_down[layer], 256)
    h = h.at[idx_p.reshape(-1)].add(ye.reshape(-1, D_MODEL))

    out = rmsnorm_rows(h, norm_final, tm)
    return out[None, N_META:]
```

```python
import functools
import math

import numpy as np
import jax
import jax.numpy as jnp
from jax import lax
from jax.experimental import pallas as pl
from jax.experimental.pallas import tpu as pltpu

D_MODEL = 2048
N_META = 16
ML_HEADS = 8
ML_HEAD_DIM = D_MODEL // ML_HEADS
ML_WIDTH = ML_HEADS * ML_HEAD_DIM
ML_CHUNK = 64
HY_WIDTH = D_MODEL
HY_ORDER = 2
HY_BANDS = 16
HY_FAST_DECAY_PCT = 0.3
HY_SLOW_DECAY_PCT = 1.5
HY_DECAY_TARGET = 1e-2
N_EXPERTS = 16
EC_CAPACITY = 2
D_EXPERT = 5504
RMS_EPS = 1e-6
NEG_GATE = -1e9

OFF_G = 4 * ML_WIDTH
OFF_HY = OFF_G + 4 * ML_HEADS
OFF_MERGE = OFF_HY + 3 * HY_WIDTH

VMEM_LIMIT_BYTES = 56 * 1024 * 1024

F32 = jnp.float32
BF16 = jnp.bfloat16


def _cparams(sem):
    return pltpu.CompilerParams(dimension_semantics=sem, vmem_limit_bytes=VMEM_LIMIT_BYTES)


def _norm_proj_kernel(x_ref, g_ref, w_ref, o_ref, u_ref):
    @pl.when(pl.program_id(1) == 0)
    def _():
        x = x_ref[...]
        y = x * lax.rsqrt(jnp.mean(x * x, axis=-1, keepdims=True) + RMS_EPS)
        u_ref[...] = (y * g_ref[...]).astype(BF16)

    o_ref[...] = jnp.dot(u_ref[...], w_ref[...].astype(BF16), preferred_element_type=F32)


def norm_proj(x, g, w, n_cols, col_block0, tm, tn):
    m, d = x.shape
    grid = (m // tm, n_cols // tn)
    return pl.pallas_call(
        _norm_proj_kernel,
        grid=grid,
        in_specs=[
            pl.BlockSpec((tm, d), lambda i, j: (i, 0)),
            pl.BlockSpec((1, d), lambda i, j: (0, 0)),
            pl.BlockSpec((d, tn), lambda i, j: (0, j + col_block0)),
        ],
        out_specs=pl.BlockSpec((tm, tn), lambda i, j: (i, j)),
        out_shape=jax.ShapeDtypeStruct((m, n_cols), F32),
        scratch_shapes=[pltpu.VMEM((tm, d), BF16)],
        compiler_params=_cparams(("parallel", "arbitrary")),
        name="norm_proj",
    )(x, g.reshape(1, d), w)


def _gated_dual_mm_kernel(ya_ref, yb_ref, pa_ref, pb_ref, wa_ref, wb_ref, o_ref):
    za = jnp.dot(ya_ref[...], wa_ref[...].astype(BF16), preferred_element_type=F32)
    zb = jnp.dot(yb_ref[...], wb_ref[...].astype(BF16), preferred_element_type=F32)
    merged = jax.nn.sigmoid(pa_ref[...]) * za + jax.nn.sigmoid(pb_ref[...]) * zb
    o_ref[...] = merged.astype(o_ref.dtype)


def gated_dual_mm(ya, yb, p_merge, gate_col0, wa, wb, tm, tn):
    m, d = ya.shape
    nb = d // tn
    g0 = gate_col0 // tn
    return pl.pallas_call(
        _gated_dual_mm_kernel,
        grid=(m // tm, nb),
        in_specs=[pl.BlockSpec((tm, d), lambda i, j: (i, 0)),
                  pl.BlockSpec((tm, d), lambda i, j: (i, 0)),
                  pl.BlockSpec((tm, tn), lambda i, j: (i, j + g0)),
                  pl.BlockSpec((tm, tn), lambda i, j: (i, j + g0 + nb)),
                  pl.BlockSpec((d, tn), lambda i, j: (0, j)),
                  pl.BlockSpec((d, tn), lambda i, j: (0, j))],
        out_specs=pl.BlockSpec((tm, tn), lambda i, j: (i, j)),
        out_shape=jax.ShapeDtypeStruct((m, d), BF16),
        compiler_params=_cparams(("parallel", "arbitrary")),
        name="gated_dual_mm",
    )(ya, yb, p_merge, p_merge, wa, wb)


def _mm_res_kernel(a_ref, b_ref, r_ref, o_ref):
    o_ref[...] = r_ref[...] + jnp.dot(a_ref[...], b_ref[...].astype(BF16), preferred_element_type=F32)


def mm_residual(a, b, res, tm, tn):
    m, k = a.shape
    _, n = b.shape
    return pl.pallas_call(
        _mm_res_kernel,
        grid=(m // tm, n // tn),
        in_specs=[pl.BlockSpec((tm, k), lambda i, j: (i, 0)),
                  pl.BlockSpec((k, tn), lambda i, j: (0, j)),
                  pl.BlockSpec((tm, tn), lambda i, j: (i, j))],
        out_specs=pl.BlockSpec((tm, tn), lambda i, j: (i, j)),
        out_shape=jax.ShapeDtypeStruct((m, n), F32),
        compiler_params=_cparams(("parallel", "arbitrary")),
        name="mm_residual",
    )(a, b, res)


def _ffn_kernel(xe_ref, wg_ref, wu_ref, wd_ref, o_ref, acc_ref, *, tf, f_total):
    f = pl.program_id(1)
    x = xe_ref[0]
    a = jnp.dot(x, wg_ref[0].astype(BF16), preferred_element_type=F32)
    b = jnp.dot(x, wu_ref[0].astype(BF16), preferred_element_type=F32)
    hid = (a * jax.nn.sigmoid(a)) * b
    col_ok = (f * tf + lax.broadcasted_iota(jnp.int32, (1, tf), 1)) < f_total
    row_ok = (f * tf + lax.broadcasted_iota(jnp.int32, (tf, 1), 0)) < f_total
    hid = jnp.where(col_ok, hid, 0.0).astype(BF16)
    wd = jnp.where(row_ok, wd_ref[0], 0.0).astype(BF16)
    y = jnp.dot(hid, wd, preferred_element_type=F32)

    @pl.when(f == 0)
    def _():
        acc_ref[...] = y

    @pl.when(f > 0)
    def _():
        acc_ref[...] += y

    @pl.when(f == pl.num_programs(1) - 1)
    def _():
        o_ref[0] = acc_ref[...].astype(o_ref.dtype)


def expert_ffn(xe, wg, wu, wd, tf):
    e, c, d = xe.shape
    f_total = wg.shape[-1]
    nf = pl.cdiv(f_total, tf)
    return pl.pallas_call(
        functools.partial(_ffn_kernel, tf=tf, f_total=f_total),
        grid=(e, nf),
        in_specs=[pl.BlockSpec((1, c, d), lambda i, f: (i, 0, 0)),
                  pl.BlockSpec((1, d, tf), lambda i, f: (i, 0, f)),
                  pl.BlockSpec((1, d, tf), lambda i, f: (i, 0, f)),
                  pl.BlockSpec((1, tf, d), lambda i, f: (i, f, 0))],
        out_specs=pl.BlockSpec((1, c, d), lambda i, f: (i, 0, 0)),
        out_shape=jax.ShapeDtypeStruct((e, c, d), BF16),
        scratch_shapes=[pltpu.VMEM((c, d), F32)],
        compiler_params=_cparams(("parallel", "arbitrary")),
        name="expert_ffn",
    )(xe, wg, wu, wd)


FFT_N1 = 256
FFT_N2 = 72
FFT_N = FFT_N1 * FFT_N2
SEQ_SLABS = 128
SEQ_PAD = SEQ_SLABS * FFT_N2
LANES = 128
K1_CHUNK = 32
K1_UNROLL = 8
N2_UNROLL = 3
HY_ORDERS = 2
FILT_ROWS = 1152
HIGHEST = lax.Precision.HIGHEST


def fft_tables():
    n1 = np.arange(FFT_N1)
    ang1 = 2.0 * np.pi * np.outer(n1, n1) / FFT_N1
    c1, s1 = np.cos(ang1), np.sin(ang1)
    f1_fwd = np.concatenate([c1, -s1], axis=0)
    f1_inv = np.concatenate([c1, -s1], axis=1) / FFT_N
    k2 = np.arange(FFT_N2)
    n2 = np.arange(FFT_N2)
    k1 = np.arange(FFT_N1)
    ang = 2.0 * np.pi * (n2[None, None, :] * (k1[:, None, None] + FFT_N1 * k2[None, :, None])) / FFT_N
    g = np.concatenate([np.cos(ang), -np.sin(ang)], axis=2)
    angt = np.transpose(ang, (0, 2, 1))
    h = np.concatenate([np.cos(angt), -np.sin(angt)], axis=2)
    return (jnp.asarray(f1_fwd, F32), jnp.asarray(f1_inv, F32), jnp.asarray(g, F32), jnp.asarray(h, F32))


def _stage1_fwd(x_ref, f1_ref, buf_ref, n_rows):
    f1 = f1_ref[:, 0:n_rows].astype(BF16)

    def body(j, carry):
        na = 2 * j
        sa = x_ref[pl.ds(na, n_rows, stride=FFT_N2), :]
        sb = x_ref[pl.ds(na + 1, n_rows, stride=FFT_N2), :]
        s = jnp.concatenate([sa, sb], axis=1).astype(BF16)
        u = jnp.dot(f1, s, preferred_element_type=F32)
        buf_ref[0, pl.ds(na, FFT_N1, stride=FFT_N2), :] = u[0:FFT_N1, 0:LANES]
        buf_ref[0, pl.ds(na + 1, FFT_N1, stride=FFT_N2), :] = u[0:FFT_N1, LANES:]
        buf_ref[1, pl.ds(na, FFT_N1, stride=FFT_N2), :] = u[FFT_N1:, 0:LANES]
        buf_ref[1, pl.ds(na + 1, FFT_N1, stride=FFT_N2), :] = u[FFT_N1:, LANES:]
        return carry

    lax.fori_loop(0, FFT_N2 // 2, body, 0, unroll=N2_UNROLL)


def _stage2_fwd(buf_ref, g_ref, row0, kk):
    ar = buf_ref[0, pl.ds(row0, FFT_N2), :]
    ai = buf_ref[1, pl.ds(row0, FFT_N2), :]
    t = jnp.concatenate([jnp.concatenate([ar, ai], axis=1),
                         jnp.concatenate([-ai, ar], axis=1)], axis=0).astype(BF16)
    x = jnp.dot(g_ref[kk].astype(BF16), t, preferred_element_type=F32)
    return x[:, 0:LANES], x[:, LANES:]


def _spectrum_kernel(k_ref, f1_ref, g_ref, o_ref, buf_ref):
    c = pl.program_id(2)

    @pl.when(c == 0)
    def _():
        _stage1_fwd(k_ref, f1_ref, buf_ref, FFT_N1)

    def body(kk, carry):
        row0 = pl.multiple_of((c * K1_CHUNK + kk) * FFT_N2, 8)
        xr, xi = _stage2_fwd(buf_ref, g_ref, row0, kk)
        o0 = pl.multiple_of(kk * FFT_N2, 8)
        o_ref[0, pl.ds(o0, FFT_N2), :] = xr
        o_ref[1, pl.ds(o0, FFT_N2), :] = xi
        return carry

    lax.fori_loop(0, K1_CHUNK, body, 0, unroll=K1_UNROLL)


def filter_spectrum(kern, f1_fwd, g):
    no, n, ch = kern.shape
    nchunk = FFT_N1 // K1_CHUNK
    return pl.pallas_call(
        _spectrum_kernel,
        grid=(no, ch // LANES, nchunk),
        in_specs=[pl.BlockSpec((None, n, LANES), lambda o, b, c: (o, 0, b)),
                  pl.BlockSpec((2 * FFT_N1, FFT_N1), lambda o, b, c: (0, 0)),
                  pl.BlockSpec((K1_CHUNK, FFT_N2, 2 * FFT_N2), lambda o, b, c: (c, 0, 0))],
        out_specs=pl.BlockSpec((None, 2, K1_CHUNK * FFT_N2, LANES), lambda o, b, c: (o, 0, c, b)),
        out_shape=jax.ShapeDtypeStruct((no, 2, n, ch), F32),
        scratch_shapes=[pltpu.VMEM((2, FFT_N, LANES), F32)],
        compiler_params=_cparams(("parallel", "parallel", "arbitrary")),
        name="filter_spectrum",
    )(kern, f1_fwd, g)


def _conv_kernel(z_ref, kf_ref, f1_ref, f1i_ref, g_ref, h_ref, o_ref, buf_ref):
    c = pl.program_id(1)

    @pl.when(c == 0)
    def _():
        _stage1_fwd(z_ref, f1_ref, buf_ref, SEQ_SLABS)

    def body(kk, carry):
        row0 = pl.multiple_of((c * K1_CHUNK + kk) * FFT_N2, 8)
        xr, xi = _stage2_fwd(buf_ref, g_ref, row0, kk)
        k0 = pl.multiple_of(kk * FFT_N2, 8)
        kr = kf_ref[0, pl.ds(k0, FFT_N2), :]
        ki = kf_ref[1, pl.ds(k0, FFT_N2), :]
        pr = xr * kr - xi * ki
        pi = xr * ki + xi * kr
        t = jnp.concatenate([jnp.concatenate([pr, pi], axis=1),
                             jnp.concatenate([pi, -pr], axis=1)], axis=0).astype(BF16)
        q = jnp.dot(h_ref[kk].astype(BF16), t, preferred_element_type=F32)
        buf_ref[0, pl.ds(row0, FFT_N2), :] = q[:, 0:LANES]
        buf_ref[1, pl.ds(row0, FFT_N2), :] = q[:, LANES:]
        return carry

    lax.fori_loop(0, K1_CHUNK, body, 0, unroll=K1_UNROLL)

    @pl.when(c == pl.num_programs(1) - 1)
    def _():
        f1i = f1i_ref[0:SEQ_SLABS, :].astype(BF16)

        def inv_body(j, carry):
            na = 2 * j
            sra = buf_ref[0, pl.ds(na, FFT_N1, stride=FFT_N2), :]
            srb = buf_ref[0, pl.ds(na + 1, FFT_N1, stride=FFT_N2), :]
            sia = buf_ref[1, pl.ds(na, FFT_N1, stride=FFT_N2), :]
            sib = buf_ref[1, pl.ds(na + 1, FFT_N1, stride=FFT_N2), :]
            s = jnp.concatenate([jnp.concatenate([sra, srb], axis=1),
                                 jnp.concatenate([sia, sib], axis=1)], axis=0).astype(BF16)
            y = jnp.dot(f1i, s, preferred_element_type=F32)
            o_ref[pl.ds(na, SEQ_SLABS, stride=FFT_N2), :] = y[:, 0:LANES]
            o_ref[pl.ds(na + 1, SEQ_SLABS, stride=FFT_N2), :] = y[:, LANES:]
            return carry

        lax.fori_loop(0, FFT_N2 // 2, inv_body, 0, unroll=N2_UNROLL)


def long_conv(z_src, z_block0, ch, kf, order, tables):
    f1_fwd, f1_inv, g, h = tables
    rows = z_src.shape[0]
    assert rows == SEQ_PAD
    nchunk = FFT_N1 // K1_CHUNK
    return pl.pallas_call(
        _conv_kernel,
        grid=(ch // LANES, nchunk),
        in_specs=[pl.BlockSpec((rows, LANES), lambda b, c: (0, b + z_block0)),
                  pl.BlockSpec((None, 2, K1_CHUNK * FFT_N2, LANES), lambda b, c: (order, 0, c, b)),
                  pl.BlockSpec((2 * FFT_N1, FFT_N1), lambda b, c: (0, 0)),
                  pl.BlockSpec((FFT_N1, 2 * FFT_N1), lambda b, c: (0, 0)),
                  pl.BlockSpec((K1_CHUNK, FFT_N2, 2 * FFT_N2), lambda b, c: (c, 0, 0)),
                  pl.BlockSpec((K1_CHUNK, FFT_N2, 2 * FFT_N2), lambda b, c: (c, 0, 0))],
        out_specs=pl.BlockSpec((rows, LANES), lambda b, c: (0, b)),
        out_shape=jax.ShapeDtypeStruct((rows, ch), F32),
        scratch_shapes=[pltpu.VMEM((2, FFT_N, LANES), F32)],
        compiler_params=_cparams(("parallel", "arbitrary")),
        name="long_conv",
    )(z_src, kf, f1_fwd, f1_inv, g, h)


def _filter_kernel(emb_ref, w1_ref, b1_ref, w2_ref, b2_ref, fr_ref, w3_ref, dl_ref, o_ref, h_ref, *, seq_len):
    i = pl.program_id(0)

    @pl.when((pl.program_id(1) == 0) & (pl.program_id(2) == 0))
    def _():
        fr = fr_ref[...]
        h1 = jnp.sin(fr * (jnp.dot(emb_ref[...], w1_ref[...], precision=HIGHEST, preferred_element_type=F32)
                           + b1_ref[...]))
        h_ref[...] = jnp.sin(fr * (jnp.dot(h1, w2_ref[...], precision=HIGHEST, preferred_element_type=F32)
                                   + b2_ref[...]))

    h2 = h_ref[...]
    h_hi = h2.astype(BF16)
    h_lo = (h2 - h_hi.astype(F32)).astype(BF16)
    w3 = w3_ref[...]
    w_hi = w3.astype(BF16)
    w_lo = (w3 - w_hi.astype(F32)).astype(BF16)
    f = (jnp.dot(h_hi, w_hi, preferred_element_type=F32)
         + (jnp.dot(h_hi, w_lo, preferred_element_type=F32) + jnp.dot(h_lo, w_hi, preferred_element_type=F32)))
    t = emb_ref[:, 0:1]
    f = f * jnp.exp(-t * dl_ref[...])
    r = i * FILT_ROWS + lax.broadcasted_iota(jnp.int32, (FILT_ROWS, 1), 0)
    fwd = i < (FFT_N // 2) // FILT_ROWS
    lo = jnp.where(fwd, -1, FFT_N - seq_len)
    hi = jnp.where(fwd, seq_len, FFT_N)
    valid = (r > lo) & (r < hi)
    o_ref[0] = jnp.where(valid, f, 0.0)


def filter_kernels(emb, w1p, b1, w2, b2, freq, w3, deltas, seq_len, tc=512):
    ch = deltas.shape[0]
    hid = w2.shape[0]
    ncb = ch // tc
    half = (FFT_N // 2) // FILT_ROWS
    full = lambda i, o, j: (0, 0)
    return pl.pallas_call(
        functools.partial(_filter_kernel, seq_len=seq_len),
        grid=(FFT_N // FILT_ROWS, HY_ORDERS, ncb),
        in_specs=[pl.BlockSpec((FILT_ROWS, emb.shape[1]), lambda i, o, j: (i, 0)),
                  pl.BlockSpec(w1p.shape, full), pl.BlockSpec((1, hid), full),
                  pl.BlockSpec((hid, hid), full), pl.BlockSpec((1, hid), full), pl.BlockSpec((1, hid), full),
                  pl.BlockSpec((hid, tc), lambda i, o, j: (0, (2 * o + i // half) * ncb + j)),
                  pl.BlockSpec((1, tc), lambda i, o, j: (0, j))],
        out_specs=pl.BlockSpec((1, FILT_ROWS, tc), lambda i, o, j: (o, i, j)),
        out_shape=jax.ShapeDtypeStruct((HY_ORDERS, FFT_N, ch), F32),
        scratch_shapes=[pltpu.VMEM((FILT_ROWS, hid), F32)],
        compiler_params=_cparams(("parallel", "arbitrary", "arbitrary")),
        name="hyena_filters",
    )(emb, w1p, b1.reshape(1, hid), w2, b2.reshape(1, hid), freq.reshape(1, hid), w3, deltas.reshape(1, ch))


def _short_conv_kernel(u_ref, w_ref, o_ref):
    u = u_ref[...]
    rows = u.shape[0]
    r = lax.broadcasted_iota(jnp.int32, (rows, 1), 0)
    prev = jnp.where(r == 0, 0.0, pltpu.roll(u, 1, axis=0))
    nxt = jnp.where(r == rows - 1, 0.0, pltpu.roll(u, rows - 1, axis=0))
    w = w_ref[...]
    o_ref[0:rows, :] = w[0:1, :] * prev + w[1:2, :] * u + w[2:3, :] * nxt
    o_ref[rows:, :] = jnp.zeros((o_ref.shape[0] - rows, o_ref.shape[1]), F32)


def short_conv(p, conv_w, n_cols, col_block0=0):
    rows = p.shape[0]
    return pl.pallas_call(
        _short_conv_kernel,
        grid=(n_cols // LANES,),
        in_specs=[pl.BlockSpec((rows, LANES), lambda j: (0, j + col_block0)),
                  pl.BlockSpec((3, LANES), lambda j: (0, j))],
        out_specs=pl.BlockSpec((SEQ_PAD, LANES), lambda j: (0, j)),
        out_shape=jax.ShapeDtypeStruct((SEQ_PAD, n_cols), F32),
        compiler_params=_cparams(("parallel",)),
        name="short_conv",
    )(p, conv_w)


def _gate_kernel(g_ref, c_ref, z_ref, b_ref, o_ref):
    o_ref[...] = (g_ref[...] * (c_ref[...] + b_ref[...] * z_ref[...])).astype(o_ref.dtype)


def hyena_gate(uc, gate_block0, conv, z_src, z_block0, bias_row, rows, out_dtype, tr, tc=512):
    ch = conv.shape[1]
    return pl.pallas_call(
        _gate_kernel,
        grid=(rows // tr, ch // tc),
        in_specs=[pl.BlockSpec((tr, tc), lambda i, j: (i, j + gate_block0)),
                  pl.BlockSpec((tr, tc), lambda i, j: (i, j)),
                  pl.BlockSpec((tr, tc), lambda i, j: (i, j + z_block0)),
                  pl.BlockSpec((1, tc), lambda i, j: (0, j))],
        out_specs=pl.BlockSpec((tr, tc), lambda i, j: (i, j)),
        out_shape=jax.ShapeDtypeStruct((rows, ch), out_dtype),
        compiler_params=_cparams(("parallel", "parallel")),
        name="hyena_gate",
    )(uc, conv, z_src, bias_row)


def position_features(seq_len, n_bands):
    r = np.arange(FFT_N)
    pos = jnp.asarray(np.where(r < FFT_N // 2, r, FFT_N - r).astype(np.float32)[:, None])
    t = pos / (seq_len - 1)
    bands = jnp.linspace(1e-4, n_bands - 1, n_bands, dtype=F32)[None, :]
    ang = bands * (2.0 * math.pi) * pos / seq_len
    emb = jnp.concatenate([t, jnp.cos(ang), -jnp.sin(ang)], axis=-1)
    return jnp.pad(emb, ((0, 0), (0, LANES - emb.shape[1])))


def hyena_branch(p, col_block0, conv_w, w1, b1, w2, b2, freq, w3, bias, deltas, seq_len, n_bands, out_dtype):
    ch = deltas.shape[0]
    assert seq_len % FFT_N2 == 0 and seq_len // FFT_N2 <= SEQ_SLABS and 2 * seq_len - 1 <= FFT_N
    tables = fft_tables()
    emb = position_features(seq_len, n_bands)
    w1p = jnp.pad(w1, ((0, LANES - w1.shape[0]), (0, 0)))
    kern = filter_kernels(emb, w1p, b1, w2, b2, freq, w3, deltas, seq_len, tc=min(512, ch))
    kf = filter_spectrum(kern, tables[0], tables[2])
    uc = short_conv(p, conv_w, 3 * ch, col_block0)
    conv0 = long_conv(uc, 0, ch, kf, 0, tables)
    tcg = min(512, ch)
    gb = ch // tcg
    z1 = hyena_gate(uc, gb, conv0, uc, 0, bias[0:1], SEQ_PAD, F32, FILT_ROWS, tcg)
    conv1 = long_conv(z1, 0, ch, kf, 1, tables)
    return hyena_gate(uc, 2 * gb, conv1, z1, 0, bias[1:2], seq_len, out_dtype, 912, tcg)


def _mlstm_direction(d, chunk, q_ref, k_ref, v_ref, g_ref, bias_ref, tri_ref, o_ref, c_ref, n_ref, m_ref, seq_len):
    t_idx = lax.broadcasted_iota(jnp.int32, (ML_CHUNK, 1), 0)
    valid = t_idx < (seq_len - chunk * ML_CHUNK)
    g = g_ref[...] + bias_ref[...]
    li_all = jnp.where(valid, g[:, 16 * d:16 * d + 8], NEG_GATE)
    lf_all = jnp.where(valid, jax.nn.log_sigmoid(g[:, 16 * d + 8:16 * d + 16]), 0.0)
    cum_all = jnp.dot(tri_ref[d], lf_all, precision=HIGHEST, preferred_element_type=F32)
    tot_all = jnp.sum(lf_all, axis=0, keepdims=True)
    a_all = tot_all - cum_all + li_all
    amax_all = jnp.max(a_all, axis=0, keepdims=True)
    tt = lax.broadcasted_iota(jnp.int32, (ML_CHUNK, ML_CHUNK), 0)
    ss = lax.broadcasted_iota(jnp.int32, (ML_CHUNK, ML_CHUNK), 1)
    visible = (ss <= tt) if d == 0 else (ss >= tt)
    ones = jnp.ones((ML_CHUNK, 1), F32)
    scale = ML_HEAD_DIM ** -0.5
    for h in range(ML_HEADS):
        sl = slice(h * ML_HEAD_DIM, (h + 1) * ML_HEAD_DIM)
        q = jnp.where(valid, q_ref[:, sl], 0.0)
        k = jnp.where(valid, k_ref[:, sl], 0.0) * scale
        v = jnp.where(valid, v_ref[:, sl], 0.0)
        qb, kb, vb = q.astype(BF16), k.astype(BF16), v.astype(BF16)
        cum = cum_all[:, h:h + 1]
        li = li_all[:, h:h + 1]
        tot = tot_all[:, h:h + 1]
        st = d * ML_HEADS + h
        m_prev = m_ref[st]
        c_prev = c_ref[st]
        n_prev = n_ref[st]
        lhs = jnp.concatenate([cum, ones], axis=1)
        rhs = jnp.concatenate([ones, li - cum], axis=1)
        log_d = lax.dot_general(lhs, rhs, (((1,), (1,)), ((), ())), precision=HIGHEST,
                                preferred_element_type=F32)
        log_d = jnp.where(visible, log_d, -jnp.inf)
        log_inter = cum + m_prev
        m_t = jnp.maximum(log_inter, jnp.max(log_d, axis=1, keepdims=True))
        s = lax.dot_general(qb, kb, (((1,), (1,)), ((), ())), preferred_element_type=F32) * jnp.exp(log_d - m_t)
        w_inter = jnp.exp(log_inter - m_t)
        num = (jnp.dot(s.astype(BF16), vb, preferred_element_type=F32)
               + w_inter * jnp.dot(qb, c_prev.astype(BF16), preferred_element_type=F32))
        den = jnp.sum(s, axis=1, keepdims=True) + w_inter * jnp.sum(q * n_prev, axis=1, keepdims=True)
        o_ref[:, sl] = num / jnp.maximum(jnp.abs(den), jnp.exp(-m_t))
        m_new = jnp.maximum(tot + m_prev, amax_all[:, h:h + 1])
        decay = jnp.exp(tot + m_prev - m_new)
        kw = k * jnp.exp(a_all[:, h:h + 1] - m_new)
        c_ref[st] = decay * c_prev + lax.dot_general(kw.astype(BF16), vb, (((0,), (0,)), ((), ())),
                                                     preferred_element_type=F32)
        n_ref[st] = decay * n_prev + jnp.sum(kw, axis=0, keepdims=True)
        m_ref[st] = m_new


def _mlstm_kernel(qf, kf, vf, gf, qb, kb, vb, gb, bias_ref, tri_ref, of_ref, ob_ref, c_ref, n_ref, m_ref,
                  *, n_chunks, seq_len):
    i = pl.program_id(0)

    @pl.when(i == 0)
    def _():
        c_ref[...] = jnp.zeros(c_ref.shape, F32)
        n_ref[...] = jnp.zeros(n_ref.shape, F32)
        m_ref[...] = jnp.zeros(m_ref.shape, F32)

    _mlstm_direction(0, i, qf, kf, vf, gf, bias_ref, tri_ref, of_ref, c_ref, n_ref, m_ref, seq_len)
    _mlstm_direction(1, n_chunks - 1 - i, qb, kb, vb, gb, bias_ref, tri_ref, ob_ref, c_ref, n_ref, m_ref, seq_len)


def mlstm_scan(qkvo, p_gate, gate_bias):
    seq_len = qkvo.shape[0]
    w = ML_WIDTH
    n_chunks = pl.cdiv(seq_len, ML_CHUNK)
    tri = np.tril(np.ones((ML_CHUNK, ML_CHUNK), np.float32))
    tri = jnp.asarray(np.stack([tri, tri.T]))
    fw = lambda col: (lambda i: (i, col))
    bw = lambda col: (lambda i: (n_chunks - 1 - i, col))
    blk = lambda im: pl.BlockSpec((ML_CHUNK, w), im)
    gblk = lambda im: pl.BlockSpec((ML_CHUNK, 4 * ML_HEADS), im)
    return pl.pallas_call(
        functools.partial(_mlstm_kernel, n_chunks=n_chunks, seq_len=seq_len),
        grid=(n_chunks,),
        in_specs=[blk(fw(0)), blk(fw(1)), blk(fw(2)), gblk(fw(0)),
                  blk(bw(0)), blk(bw(1)), blk(bw(2)), gblk(bw(0)),
                  pl.BlockSpec((1, 4 * ML_HEADS), lambda i: (0, 0)),
                  pl.BlockSpec((2, ML_CHUNK, ML_CHUNK), lambda i: (0, 0, 0))],
        out_specs=[blk(fw(0)), blk(bw(0))],
        out_shape=[jax.ShapeDtypeStruct((seq_len, w), F32), jax.ShapeDtypeStruct((seq_len, w), F32)],
        scratch_shapes=[pltpu.VMEM((2 * ML_HEADS, ML_HEAD_DIM, ML_HEAD_DIM), F32),
                        pltpu.VMEM((2 * ML_HEADS, 1, ML_HEAD_DIM), F32),
                        pltpu.VMEM((2 * ML_HEADS, 1, 1), F32)],
        compiler_params=_cparams(("arbitrary",)),
        name="mlstm_scan",
    )(qkvo, qkvo, qkvo, p_gate, qkvo, qkvo, qkvo, p_gate, gate_bias.reshape(1, -1), tri)


def _mlstm_finish_kernel(hf_ref, hb_ref, o_ref, g_ref, y_ref):
    for h in range(ML_HEADS):
        sl = slice(h * ML_HEAD_DIM, (h + 1) * ML_HEAD_DIM)
        x = hf_ref[:, sl] + hb_ref[:, sl]
        hn = x * lax.rsqrt(jnp.mean(x * x, axis=-1, keepdims=True) + RMS_EPS) * g_ref[:, sl]
        y_ref[:, sl] = (jax.nn.sigmoid(o_ref[:, sl]) * hn).astype(y_ref.dtype)


def mlstm_finish(h_fw, h_bw, qkvo, head_norm, tm, out_dtype):
    seq_len, w = h_fw.shape
    row = lambda i: (i, 0)
    return pl.pallas_call(
        _mlstm_finish_kernel,
        grid=(seq_len // tm,),
        in_specs=[pl.BlockSpec((tm, w), row), pl.BlockSpec((tm, w), row),
                  pl.BlockSpec((tm, w), lambda i: (i, 3)), pl.BlockSpec((1, w), lambda i: (0, 0))],
        out_specs=pl.BlockSpec((tm, w), row),
        out_shape=jax.ShapeDtypeStruct((seq_len, w), out_dtype),
        compiler_params=_cparams(("parallel",)),
        name="mlstm_finish",
    )(h_fw, h_bw, qkvo, head_norm.reshape(1, w))


TOK_TILE = 256
WIN = 272
I32 = jnp.int32


def _router_kernel(h_ref, g_ref, w_ref, u_ref, a_ref):
    x = h_ref[...]
    u = (x * lax.rsqrt(jnp.mean(x * x, axis=-1, keepdims=True) + RMS_EPS) * g_ref[...]).astype(BF16)
    u_ref[...] = u
    logits = jnp.dot(u, w_ref[...].astype(BF16), preferred_element_type=F32)
    lane = lax.broadcasted_iota(I32, logits.shape, 1)
    logits = jnp.where(lane < N_EXPERTS, logits, -jnp.inf)
    z = jnp.exp(logits - jnp.max(logits, axis=-1, keepdims=True))
    a_ref[...] = z / jnp.sum(z, axis=-1, keepdims=True)


def router(h, g, w_router, tm):
    m, d = h.shape
    wp = jnp.pad(w_router, ((0, 0), (0, LANES - w_router.shape[1])))
    return pl.pallas_call(
        _router_kernel,
        grid=(m // tm,),
        in_specs=[pl.BlockSpec((tm, d), lambda i: (i, 0)), pl.BlockSpec((1, d), lambda i: (0, 0)),
                  pl.BlockSpec((d, LANES), lambda i: (0, 0))],
        out_specs=[pl.BlockSpec((tm, d), lambda i: (i, 0)), pl.BlockSpec((tm, LANES), lambda i: (i, 0))],
        out_shape=[jax.ShapeDtypeStruct((m, d), BF16), jax.ShapeDtypeStruct((m, LANES), F32)],
        compiler_params=_cparams(("parallel",)),
        name="router",
    )(h, g.reshape(1, d), wp)


def _topk_kernel(a_ref, tri_ref, eye_ref, posc_ref, posr_ref, start_ref, *, cap, n_tiles):
    bits = lax.bitcast_convert_type(a_ref[...], I32)

    def search(b, lo):
        cand = lo | (jnp.int32(1) << (30 - b))
        cnt = jnp.sum((bits >= cand).astype(I32), axis=0, keepdims=True)
        return jnp.where(cnt >= cap, cand, lo)

    thr = lax.fori_loop(0, 31, search, jnp.zeros((1, LANES), I32))
    n_gt = jnp.sum((bits > thr).astype(I32), axis=0, keepdims=True)
    need_eq = (cap - n_gt).astype(F32)
    tri = tri_ref[...]
    eye = eye_ref[...]

    def tile(t, carry):
        eq_before, sel_before = carry
        r0 = pl.multiple_of(t * TOK_TILE, TOK_TILE)
        b = lax.bitcast_convert_type(a_ref[pl.ds(r0, TOK_TILE), :], I32)
        gt = b > thr
        eq = b == thr
        eq_rank = eq_before + jnp.dot(tri, eq.astype(BF16), preferred_element_type=F32)
        sel = gt | (eq & (eq_rank < need_eq))
        pos = sel_before + jnp.dot(tri, sel.astype(BF16), preferred_element_type=F32)
        posm = jnp.where(sel, pos, -1.0)
        posc_ref[pl.ds(r0, TOK_TILE), :] = posm
        posr_ref[:, pl.ds(r0, TOK_TILE)] = lax.dot_general(eye, posm, (((1,), (1,)), ((), ())), precision=HIGHEST,
                                                           preferred_element_type=F32)
        start_ref[pl.ds(t, 1), :] = sel_before
        return (eq_before + jnp.sum(eq.astype(F32), axis=0, keepdims=True),
                sel_before + jnp.sum(sel.astype(F32), axis=0, keepdims=True))

    lax.fori_loop(0, n_tiles, tile, (jnp.zeros((1, LANES), F32), jnp.zeros((1, LANES), F32)))


def expert_choice_topk(aff_pad, cap):
    rows = aff_pad.shape[0]
    n_tiles = rows // TOK_TILE
    tri = jnp.asarray(np.tril(np.ones((TOK_TILE, TOK_TILE), np.float32), -1), BF16)
    eye = jnp.asarray(np.eye(LANES, dtype=np.float32))
    return pl.pallas_call(
        functools.partial(_topk_kernel, cap=cap, n_tiles=n_tiles),
        out_shape=[jax.ShapeDtypeStruct((rows, LANES), F32), jax.ShapeDtypeStruct((LANES, rows), F32),
                   jax.ShapeDtypeStruct((n_tiles, LANES), F32)],
        compiler_params=pltpu.CompilerParams(vmem_limit_bytes=VMEM_LIMIT_BYTES),
        name="expert_choice_topk",
    )(aff_pad, tri, eye)


def _gather_kernel(start_ref, u_ref, pr_ref, o_ref, acc_ref, *, seq_len, cap_pad):
    e = pl.program_id(0)
    t = pl.program_id(1)

    @pl.when(t == 0)
    def _():
        acc_ref[...] = jnp.zeros(acc_ref.shape, F32)

    ws = pl.multiple_of((start_ref[e, t] // 16) * 16, 16)
    row = t * TOK_TILE + lax.broadcasted_iota(I32, (TOK_TILE, 1), 0)
    u = jnp.where(row < seq_len, u_ref[...], 0)
    j = (ws + lax.broadcasted_iota(I32, (WIN, 1), 0)).astype(F32)
    onehot = (j == pr_ref[0]).astype(BF16)
    acc_ref[pl.ds(ws, WIN), :] += jnp.dot(onehot, u, preferred_element_type=F32)

    @pl.when(t == pl.num_programs(1) - 1)
    def _():
        o_ref[0] = acc_ref[0:cap_pad, :].astype(o_ref.dtype)


def gather_tokens(u, posm_row, tile_start, cap_pad):
    seq_len, d = u.shape
    n_tiles = posm_row.shape[2] // TOK_TILE
    grid_spec = pltpu.PrefetchScalarGridSpec(
        num_scalar_prefetch=1,
        grid=(N_EXPERTS, n_tiles),
        in_specs=[pl.BlockSpec((TOK_TILE, d), lambda e, t, s: (t, 0)),
                  pl.BlockSpec((1, 1, TOK_TILE), lambda e, t, s: (e, 0, t))],
        out_specs=pl.BlockSpec((1, cap_pad, d), lambda e, t, s: (e, 0, 0)),
        scratch_shapes=[pltpu.VMEM((cap_pad + WIN, d), F32)],
    )
    return pl.pallas_call(
        functools.partial(_gather_kernel, seq_len=seq_len, cap_pad=cap_pad),
        grid_spec=grid_spec,
        out_shape=jax.ShapeDtypeStruct((N_EXPERTS, cap_pad, d), BF16),
        compiler_params=_cparams(("parallel", "arbitrary")),
        name="gather_tokens",
    )(tile_start, u, posm_row)


def _combine_kernel(start_ref, h_ref, pc_ref, a_ref, g_ref, ye_ref, o_ref, buf_ref, sem_ref, *, cap_pad):
    t = pl.program_id(0)
    total = N_EXPERTS * cap_pad

    def window_start(e):
        ws = e * cap_pad + (start_ref[e, t] // 16) * 16
        return pl.multiple_of(jnp.minimum(ws, total - WIN), 16)

    def window_copy(e, slot):
        return pltpu.make_async_copy(ye_ref.at[pl.ds(window_start(e), WIN), :], buf_ref.at[slot], sem_ref.at[slot])

    window_copy(0, 0).start()
    acc = h_ref[...]
    posm = pc_ref[...]
    aff = a_ref[...]
    col = lax.broadcasted_iota(I32, (1, WIN), 1).astype(F32)
    for e in range(N_EXPERTS):
        slot = e % 2
        if e + 1 < N_EXPERTS:
            window_copy(e + 1, 1 - slot).start()
        window_copy(e, slot).wait()
        rel = posm[:, e:e + 1] + (e * cap_pad - window_start(e)).astype(F32)
        onehot = (rel == col).astype(BF16)
        y = jnp.dot(onehot, buf_ref[slot], preferred_element_type=F32)
        acc = acc + aff[:, e:e + 1] * y
    o_ref[...] = acc * lax.rsqrt(jnp.mean(acc * acc, axis=-1, keepdims=True) + RMS_EPS) * g_ref[...]


def moe_combine(h, posm_col, aff_pad, ye, tile_start, g):
    seq_len, d = h.shape
    n_tiles = posm_col.shape[0] // TOK_TILE
    cap_pad = ye.shape[1]
    grid_spec = pltpu.PrefetchScalarGridSpec(
        num_scalar_prefetch=1,
        grid=(n_tiles,),
        in_specs=[pl.BlockSpec((TOK_TILE, d), lambda t, s: (t, 0)),
                  pl.BlockSpec((TOK_TILE, LANES), lambda t, s: (t, 0)),
                  pl.BlockSpec((TOK_TILE, LANES), lambda t, s: (t, 0)),
                  pl.BlockSpec((1, d), lambda t, s: (0, 0)),
                  pl.BlockSpec(memory_space=pl.ANY)],
        out_specs=pl.BlockSpec((TOK_TILE, d), lambda t, s: (t, 0)),
        scratch_shapes=[pltpu.VMEM((2, WIN, d), BF16), pltpu.SemaphoreType.DMA((2,))],
    )
    return pl.pallas_call(
        functools.partial(_combine_kernel, cap_pad=cap_pad),
        grid_spec=grid_spec,
        out_shape=jax.ShapeDtypeStruct((seq_len, d), F32),
        compiler_params=_cparams(("arbitrary",)),
        name="moe_combine",
    )(tile_start, h, posm_col, aff_pad, g.reshape(1, d), ye.reshape(N_EXPERTS * cap_pad, d))


def kernel(x, meta_tokens, norm_mix, w_in, ml_gate_bias, ml_head_norm, hy_conv_w, hy_filt_w1, hy_filt_b1,
           hy_filt_w2, hy_filt_b2, hy_filt_freq, hy_filt_w3, hy_bias, w_branch_a, w_branch_b, w_out,
           norm_ffn, w_router, w_gate, w_up, w_down, norm_final):
    b_ = x.shape[0]
    assert b_ == 1
    h = jnp.concatenate([meta_tokens, x[0]], axis=0)
    seq_len = h.shape[0]
    tm = 912
    assert seq_len % tm == 0
    layer = 0

    w_in_l = w_in[layer]
    qkvo = norm_proj(h, norm_mix[layer], w_in_l, OFF_G, 0, tm, 1024)
    w_hm = w_in_l[:, OFF_HY:]
    p_hm = norm_proj(h, norm_mix[layer], w_hm, w_hm.shape[1], 0, tm, 1024)
    w_g = jnp.pad(w_in_l[:, OFF_G:OFF_HY], ((0, 0), (0, 128 - 4 * ML_HEADS)))
    p_g = norm_proj(h, norm_mix[layer], w_g, 128, 0, tm, 128)[:, :4 * ML_HEADS]

    h_fw, h_bw = mlstm_scan(qkvo, p_g, ml_gate_bias[layer])
    y_a = mlstm_finish(h_fw, h_bw, qkvo, ml_head_norm[layer], tm, BF16)
    max_decay = math.log(HY_DECAY_TARGET) / HY_FAST_DECAY_PCT
    min_decay = math.log(HY_DECAY_TARGET) / HY_SLOW_DECAY_PCT
    deltas = jnp.abs(jnp.linspace(min_decay, max_decay, HY_WIDTH, dtype=F32))
    y_b = hyena_branch(p_hm, 0, hy_conv_w[layer], hy_filt_w1[layer], hy_filt_b1[layer], hy_filt_w2[layer],
                       hy_filt_b2[layer], hy_filt_freq[layer], hy_filt_w3[layer], hy_bias[layer], deltas,
                       seq_len, HY_BANDS, BF16)

    merged = gated_dual_mm(y_a, y_b, p_hm, 3 * HY_WIDTH, w_branch_a[layer], w_branch_b[layer], tm, 512)
    h = mm_residual(merged, w_out[layer], h, tm, 1024)

    cap = EC_CAPACITY * seq_len // N_EXPERTS
    cap_pad = -(-cap // 16) * 16
    n_tiles = -(-seq_len // TOK_TILE)
    u, aff = router(h, norm_ffn[layer], w_router[layer], tm)
    aff_pad = jnp.pad(aff, ((0, n_tiles * TOK_TILE - seq_len), (0, 0)), constant_values=-1.0)
    posm_col, posm_row, start = expert_choice_topk(aff_pad, cap)
    tile_start = start[:, :N_EXPERTS].T.astype(jnp.int32)
    xe = gather_tokens(u, posm_row[:N_EXPERTS].reshape(N_EXPERTS, 1, -1), tile_start, cap_pad)
    ye = expert_ffn(xe, w_gate[layer], w_up[layer], w_down[layer], 256)
    out = moe_combine(h, posm_col, aff_pad, ye, tile_start, norm_final)
    return out[None, N_META:]
```

```python
import functools
import math

import numpy as np
import jax
import jax.numpy as jnp
from jax import lax
from jax.experimental import pallas as pl
from jax.experimental.pallas import tpu as pltpu

D_MODEL = 2048
N_META = 16
ML_HEADS = 8
ML_HEAD_DIM = D_MODEL // ML_HEADS
ML_WIDTH = ML_HEADS * ML_HEAD_DIM
ML_CHUNK = 256
HY_WIDTH = D_MODEL
HY_ORDER = 2
HY_BANDS = 16
HY_FAST_DECAY_PCT = 0.3
HY_SLOW_DECAY_PCT = 1.5
HY_DECAY_TARGET = 1e-2
N_EXPERTS = 16
EC_CAPACITY = 2
D_EXPERT = 5504
RMS_EPS = 1e-6
NEG_GATE = -1e9

OFF_G = 4 * ML_WIDTH
OFF_HY = OFF_G + 4 * ML_HEADS
OFF_MERGE = OFF_HY + 3 * HY_WIDTH

VMEM_LIMIT_BYTES = 56 * 1024 * 1024

F32 = jnp.float32
BF16 = jnp.bfloat16


def _cparams(sem):
    return pltpu.CompilerParams(dimension_semantics=sem, vmem_limit_bytes=VMEM_LIMIT_BYTES)


def _norm_proj_kernel(x_ref, g_ref, w_ref, o_ref, u_ref):
    @pl.when(pl.program_id(1) == 0)
    def _():
        x = x_ref[...]
        y = x * lax.rsqrt(jnp.mean(x * x, axis=-1, keepdims=True) + RMS_EPS)
        u_ref[...] = (y * g_ref[...]).astype(BF16)

    o_ref[...] = jnp.dot(u_ref[...], w_ref[...].astype(BF16), preferred_element_type=F32)


def norm_proj(x, g, w, n_cols, col_block0, tm, tn):
    m, d = x.shape
    grid = (m // tm, n_cols // tn)
    return pl.pallas_call(
        _norm_proj_kernel,
        grid=grid,
        in_specs=[
            pl.BlockSpec((tm, d), lambda i, j: (i, 0)),
            pl.BlockSpec((1, d), lambda i, j: (0, 0)),
            pl.BlockSpec((d, tn), lambda i, j: (0, j + col_block0)),
        ],
        out_specs=pl.BlockSpec((tm, tn), lambda i, j: (i, j)),
        out_shape=jax.ShapeDtypeStruct((m, n_cols), F32),
        scratch_shapes=[pltpu.VMEM((tm, d), BF16)],
        compiler_params=_cparams(("parallel", "arbitrary")),
        name="norm_proj",
    )(x, g.reshape(1, d), w)


def _gated_dual_mm_kernel(ya_ref, yb_ref, pa_ref, pb_ref, wa_ref, wb_ref, o_ref):
    za = jnp.dot(ya_ref[...], wa_ref[...].astype(BF16), preferred_element_type=F32)
    zb = jnp.dot(yb_ref[...], wb_ref[...].astype(BF16), preferred_element_type=F32)
    merged = jax.nn.sigmoid(pa_ref[...]) * za + jax.nn.sigmoid(pb_ref[...]) * zb
    o_ref[...] = merged.astype(o_ref.dtype)


def gated_dual_mm(ya, yb, p_merge, gate_col0, wa, wb, tm, tn):
    m, d = ya.shape
    nb = d // tn
    g0 = gate_col0 // tn
    return pl.pallas_call(
        _gated_dual_mm_kernel,
        grid=(m // tm, nb),
        in_specs=[pl.BlockSpec((tm, d), lambda i, j: (i, 0)),
                  pl.BlockSpec((tm, d), lambda i, j: (i, 0)),
                  pl.BlockSpec((tm, tn), lambda i, j: (i, j + g0)),
                  pl.BlockSpec((tm, tn), lambda i, j: (i, j + g0 + nb)),
                  pl.BlockSpec((d, tn), lambda i, j: (0, j)),
                  pl.BlockSpec((d, tn), lambda i, j: (0, j))],
        out_specs=pl.BlockSpec((tm, tn), lambda i, j: (i, j)),
        out_shape=jax.ShapeDtypeStruct((m, d), BF16),
        compiler_params=_cparams(("parallel", "arbitrary")),
        name="gated_dual_mm",
    )(ya, yb, p_merge, p_merge, wa, wb)


def _mm_res_kernel(a_ref, b_ref, r_ref, o_ref):
    o_ref[...] = r_ref[...] + jnp.dot(a_ref[...], b_ref[...].astype(BF16), preferred_element_type=F32)


def mm_residual(a, b, res, tm, tn):
    m, k = a.shape
    _, n = b.shape
    return pl.pallas_call(
        _mm_res_kernel,
        grid=(m // tm, n // tn),
        in_specs=[pl.BlockSpec((tm, k), lambda i, j: (i, 0)),
                  pl.BlockSpec((k, tn), lambda i, j: (0, j)),
                  pl.BlockSpec((tm, tn), lambda i, j: (i, j))],
        out_specs=pl.BlockSpec((tm, tn), lambda i, j: (i, j)),
        out_shape=jax.ShapeDtypeStruct((m, n), F32),
        compiler_params=_cparams(("parallel", "arbitrary")),
        name="mm_residual",
    )(a, b, res)


def _ffn_kernel(xe_ref, wg_ref, wu_ref, wd_ref, o_ref, acc_ref, *, tf, f_total):
    f = pl.program_id(1)
    x = xe_ref[0]
    a = jnp.dot(x, wg_ref[0].astype(BF16), preferred_element_type=F32)
    b = jnp.dot(x, wu_ref[0].astype(BF16), preferred_element_type=F32)
    hid = (a * jax.nn.sigmoid(a)) * b
    col_ok = (f * tf + lax.broadcasted_iota(jnp.int32, (1, tf), 1)) < f_total
    row_ok = (f * tf + lax.broadcasted_iota(jnp.int32, (tf, 1), 0)) < f_total
    hid = jnp.where(col_ok, hid, 0.0).astype(BF16)
    wd = jnp.where(row_ok, wd_ref[0], 0.0).astype(BF16)
    y = jnp.dot(hid, wd, preferred_element_type=F32)

    @pl.when(f == 0)
    def _():
        acc_ref[...] = y

    @pl.when(f > 0)
    def _():
        acc_ref[...] += y

    @pl.when(f == pl.num_programs(1) - 1)
    def _():
        o_ref[0] = acc_ref[...].astype(o_ref.dtype)


def expert_ffn(xe, wg, wu, wd, tf):
    e, c, d = xe.shape
    f_total = wg.shape[-1]
    nf = pl.cdiv(f_total, tf)
    return pl.pallas_call(
        functools.partial(_ffn_kernel, tf=tf, f_total=f_total),
        grid=(e, nf),
        in_specs=[pl.BlockSpec((1, c, d), lambda i, f: (i, 0, 0)),
                  pl.BlockSpec((1, d, tf), lambda i, f: (i, 0, f)),
                  pl.BlockSpec((1, d, tf), lambda i, f: (i, 0, f)),
                  pl.BlockSpec((1, tf, d), lambda i, f: (i, f, 0))],
        out_specs=pl.BlockSpec((1, c, d), lambda i, f: (i, 0, 0)),
        out_shape=jax.ShapeDtypeStruct((e, c, d), BF16),
        scratch_shapes=[pltpu.VMEM((c, d), F32)],
        compiler_params=_cparams(("parallel", "arbitrary")),
        name="expert_ffn",
    )(xe, wg, wu, wd)


FFT_N1 = 256
FFT_N2 = 72
FFT_N = FFT_N1 * FFT_N2
SEQ_SLABS = 128
SEQ_PAD = SEQ_SLABS * FFT_N2
LANES = 128
K1_HALF = 144
K1_CHUNK = 48
BUF_ROWS = K1_HALF * FFT_N2
K1_UNROLL = 8
N2_UNROLL = 3
HY_ORDERS = 2
FILT_ROWS = 1152
HIGHEST = lax.Precision.HIGHEST


def fft_tables():
    n1 = np.arange(FFT_N1)
    ang1 = 2.0 * np.pi * np.outer(n1, n1) / FFT_N1
    c1, s1 = np.cos(ang1), np.sin(ang1)
    f1_fwd = np.concatenate([c1[:K1_HALF], -s1[:K1_HALF]], axis=0)
    wgt = np.where(np.arange(K1_HALF) > FFT_N1 // 2, 0.0, 2.0)
    wgt[0] = 1.0
    wgt[FFT_N1 // 2] = 1.0
    f1_inv = np.concatenate([c1[:, :K1_HALF] * wgt, -s1[:, :K1_HALF] * wgt], axis=1) / FFT_N
    k2 = np.arange(FFT_N2)
    n2 = np.arange(FFT_N2)
    k1 = np.arange(K1_HALF)
    ang = 2.0 * np.pi * (n2[None, None, :] * (k1[:, None, None] + FFT_N1 * k2[None, :, None])) / FFT_N
    g = np.concatenate([np.cos(ang), -np.sin(ang)], axis=2)
    angt = np.transpose(ang, (0, 2, 1))
    h = np.concatenate([np.cos(angt), -np.sin(angt)], axis=2)
    return (jnp.asarray(f1_fwd, F32), jnp.asarray(f1_inv, F32), jnp.asarray(g, F32), jnp.asarray(h, F32))


def _stage1_fwd(x_ref, f1_ref, buf_ref, n_rows):
    f1 = f1_ref[:, 0:n_rows].astype(BF16)

    def body(j, carry):
        na = 2 * j
        sa = x_ref[pl.ds(na, n_rows, stride=FFT_N2), :]
        sb = x_ref[pl.ds(na + 1, n_rows, stride=FFT_N2), :]
        s = jnp.concatenate([sa, sb], axis=1).astype(BF16)
        u = jnp.dot(f1, s, preferred_element_type=F32)
        buf_ref[0, pl.ds(na, K1_HALF, stride=FFT_N2), :] = u[0:K1_HALF, 0:LANES]
        buf_ref[0, pl.ds(na + 1, K1_HALF, stride=FFT_N2), :] = u[0:K1_HALF, LANES:]
        buf_ref[1, pl.ds(na, K1_HALF, stride=FFT_N2), :] = u[K1_HALF:, 0:LANES]
        buf_ref[1, pl.ds(na + 1, K1_HALF, stride=FFT_N2), :] = u[K1_HALF:, LANES:]
        return carry

    lax.fori_loop(0, FFT_N2 // 2, body, 0, unroll=N2_UNROLL)


def _stage2_fwd(buf_ref, g_ref, row0, kk):
    ar = buf_ref[0, pl.ds(row0, FFT_N2), :]
    ai = buf_ref[1, pl.ds(row0, FFT_N2), :]
    t = jnp.concatenate([jnp.concatenate([ar, ai], axis=1),
                         jnp.concatenate([-ai, ar], axis=1)], axis=0).astype(BF16)
    x = jnp.dot(g_ref[kk].astype(BF16), t, preferred_element_type=F32)
    return x[:, 0:LANES], x[:, LANES:]


def _spectrum_kernel(k_ref, f1_ref, g_ref, o_ref, buf_ref):
    c = pl.program_id(2)

    @pl.when(c == 0)
    def _():
        _stage1_fwd(k_ref, f1_ref, buf_ref, FFT_N1)

    def body(kk, carry):
        row0 = pl.multiple_of((c * K1_CHUNK + kk) * FFT_N2, 8)
        xr, xi = _stage2_fwd(buf_ref, g_ref, row0, kk)
        o0 = pl.multiple_of(kk * FFT_N2, 8)
        o_ref[0, pl.ds(o0, FFT_N2), :] = xr
        o_ref[1, pl.ds(o0, FFT_N2), :] = xi
        return carry

    lax.fori_loop(0, K1_CHUNK, body, 0, unroll=K1_UNROLL)


def filter_spectrum(kern, f1_fwd, g):
    no, n, ch = kern.shape
    nchunk = K1_HALF // K1_CHUNK
    return pl.pallas_call(
        _spectrum_kernel,
        grid=(no, ch // LANES, nchunk),
        in_specs=[pl.BlockSpec((None, n, LANES), lambda o, b, c: (o, 0, b)),
                  pl.BlockSpec((2 * K1_HALF, FFT_N1), lambda o, b, c: (0, 0)),
                  pl.BlockSpec((K1_CHUNK, FFT_N2, 2 * FFT_N2), lambda o, b, c: (c, 0, 0))],
        out_specs=pl.BlockSpec((None, 2, K1_CHUNK * FFT_N2, LANES), lambda o, b, c: (o, 0, c, b)),
        out_shape=jax.ShapeDtypeStruct((no, 2, BUF_ROWS, ch), F32),
        scratch_shapes=[pltpu.VMEM((2, BUF_ROWS, LANES), F32)],
        compiler_params=_cparams(("parallel", "parallel", "arbitrary")),
        name="filter_spectrum",
    )(kern, f1_fwd, g)


def _conv_kernel(z_ref, kf_ref, f1_ref, f1i_ref, g_ref, h_ref, o_ref, buf_ref):
    c = pl.program_id(1)

    @pl.when(c == 0)
    def _():
        _stage1_fwd(z_ref, f1_ref, buf_ref, SEQ_SLABS)

    def body(kk, carry):
        row0 = pl.multiple_of((c * K1_CHUNK + kk) * FFT_N2, 8)
        xr, xi = _stage2_fwd(buf_ref, g_ref, row0, kk)
        k0 = pl.multiple_of(kk * FFT_N2, 8)
        kr = kf_ref[0, pl.ds(k0, FFT_N2), :]
        ki = kf_ref[1, pl.ds(k0, FFT_N2), :]
        pr = xr * kr - xi * ki
        pi = xr * ki + xi * kr
        t = jnp.concatenate([jnp.concatenate([pr, pi], axis=1),
                             jnp.concatenate([pi, -pr], axis=1)], axis=0).astype(BF16)
        q = jnp.dot(h_ref[kk].astype(BF16), t, preferred_element_type=F32)
        buf_ref[0, pl.ds(row0, FFT_N2), :] = q[:, 0:LANES]
        buf_ref[1, pl.ds(row0, FFT_N2), :] = q[:, LANES:]
        return carry

    lax.fori_loop(0, K1_CHUNK, body, 0, unroll=K1_UNROLL)

    @pl.when(c == pl.num_programs(1) - 1)
    def _():
        f1i = f1i_ref[0:SEQ_SLABS, :].astype(BF16)

        def inv_body(j, carry):
            na = 2 * j
            sra = buf_ref[0, pl.ds(na, K1_HALF, stride=FFT_N2), :]
            srb = buf_ref[0, pl.ds(na + 1, K1_HALF, stride=FFT_N2), :]
            sia = buf_ref[1, pl.ds(na, K1_HALF, stride=FFT_N2), :]
            sib = buf_ref[1, pl.ds(na + 1, K1_HALF, stride=FFT_N2), :]
            s = jnp.concatenate([jnp.concatenate([sra, srb], axis=1),
                                 jnp.concatenate([sia, sib], axis=1)], axis=0).astype(BF16)
            y = jnp.dot(f1i, s, preferred_element_type=F32)
            o_ref[pl.ds(na, SEQ_SLABS, stride=FFT_N2), :] = y[:, 0:LANES]
            o_ref[pl.ds(na + 1, SEQ_SLABS, stride=FFT_N2), :] = y[:, LANES:]
            return carry

        lax.fori_loop(0, FFT_N2 // 2, inv_body, 0, unroll=N2_UNROLL)


def long_conv(z_src, z_block0, ch, kf, order, tables):
    f1_fwd, f1_inv, g, h = tables
    rows = z_src.shape[0]
    assert rows == SEQ_PAD
    nchunk = K1_HALF // K1_CHUNK
    return pl.pallas_call(
        _conv_kernel,
        grid=(ch // LANES, nchunk),
        in_specs=[pl.BlockSpec((rows, LANES), lambda b, c: (0, b + z_block0)),
                  pl.BlockSpec((None, 2, K1_CHUNK * FFT_N2, LANES), lambda b, c: (order, 0, c, b)),
                  pl.BlockSpec((2 * K1_HALF, FFT_N1), lambda b, c: (0, 0)),
                  pl.BlockSpec((FFT_N1, 2 * K1_HALF), lambda b, c: (0, 0)),
                  pl.BlockSpec((K1_CHUNK, FFT_N2, 2 * FFT_N2), lambda b, c: (c, 0, 0)),
                  pl.BlockSpec((K1_CHUNK, FFT_N2, 2 * FFT_N2), lambda b, c: (c, 0, 0))],
        out_specs=pl.BlockSpec((rows, LANES), lambda b, c: (0, b)),
        out_shape=jax.ShapeDtypeStruct((rows, ch), F32),
        scratch_shapes=[pltpu.VMEM((2, BUF_ROWS, LANES), F32)],
        compiler_params=_cparams(("parallel", "arbitrary")),
        name="long_conv",
    )(z_src, kf, f1_fwd, f1_inv, g, h)


def _filter_kernel(emb_ref, w1_ref, b1_ref, w2_ref, b2_ref, fr_ref, w3_ref, dl_ref, o_ref, h_ref, *, seq_len):
    i = pl.program_id(0)

    @pl.when((pl.program_id(1) == 0) & (pl.program_id(2) == 0))
    def _():
        fr = fr_ref[...]
        h1 = jnp.sin(fr * (jnp.dot(emb_ref[...], w1_ref[...], precision=HIGHEST, preferred_element_type=F32)
                           + b1_ref[...]))
        h_ref[...] = jnp.sin(fr * (jnp.dot(h1, w2_ref[...], precision=HIGHEST, preferred_element_type=F32)
                                   + b2_ref[...]))

    h2 = h_ref[...]
    h_hi = h2.astype(BF16)
    h_lo = (h2 - h_hi.astype(F32)).astype(BF16)
    w3 = w3_ref[...]
    w_hi = w3.astype(BF16)
    w_lo = (w3 - w_hi.astype(F32)).astype(BF16)
    f = (jnp.dot(h_hi, w_hi, preferred_element_type=F32)
         + (jnp.dot(h_hi, w_lo, preferred_element_type=F32) + jnp.dot(h_lo, w_hi, preferred_element_type=F32)))
    t = emb_ref[:, 0:1]
    f = f * jnp.exp(-t * dl_ref[...])
    r = i * FILT_ROWS + lax.broadcasted_iota(jnp.int32, (FILT_ROWS, 1), 0)
    fwd = i < (FFT_N // 2) // FILT_ROWS
    lo = jnp.where(fwd, -1, FFT_N - seq_len)
    hi = jnp.where(fwd, seq_len, FFT_N)
    valid = (r > lo) & (r < hi)
    o_ref[0] = jnp.where(valid, f, 0.0)


def filter_kernels(emb, w1p, b1, w2, b2, freq, w3, deltas, seq_len, tc=512):
    ch = deltas.shape[0]
    hid = w2.shape[0]
    ncb = ch // tc
    half = (FFT_N // 2) // FILT_ROWS
    full = lambda i, o, j: (0, 0)
    return pl.pallas_call(
        functools.partial(_filter_kernel, seq_len=seq_len),
        grid=(FFT_N // FILT_ROWS, HY_ORDERS, ncb),
        in_specs=[pl.BlockSpec((FILT_ROWS, emb.shape[1]), lambda i, o, j: (i, 0)),
                  pl.BlockSpec(w1p.shape, full), pl.BlockSpec((1, hid), full),
                  pl.BlockSpec((hid, hid), full), pl.BlockSpec((1, hid), full), pl.BlockSpec((1, hid), full),
                  pl.BlockSpec((hid, tc), lambda i, o, j: (0, (2 * o + i // half) * ncb + j)),
                  pl.BlockSpec((1, tc), lambda i, o, j: (0, j))],
        out_specs=pl.BlockSpec((1, FILT_ROWS, tc), lambda i, o, j: (o, i, j)),
        out_shape=jax.ShapeDtypeStruct((HY_ORDERS, FFT_N, ch), F32),
        scratch_shapes=[pltpu.VMEM((FILT_ROWS, hid), F32)],
        compiler_params=_cparams(("parallel", "arbitrary", "arbitrary")),
        name="hyena_filters",
    )(emb, w1p, b1.reshape(1, hid), w2, b2.reshape(1, hid), freq.reshape(1, hid), w3, deltas.reshape(1, ch))


def _short_conv_kernel(u_ref, w_ref, o_ref):
    u = u_ref[...]
    rows = u.shape[0]
    r = lax.broadcasted_iota(jnp.int32, (rows, 1), 0)
    prev = jnp.where(r == 0, 0.0, pltpu.roll(u, 1, axis=0))
    nxt = jnp.where(r == rows - 1, 0.0, pltpu.roll(u, rows - 1, axis=0))
    w = w_ref[...]
    o_ref[0:rows, :] = w[0:1, :] * prev + w[1:2, :] * u + w[2:3, :] * nxt
    o_ref[rows:, :] = jnp.zeros((o_ref.shape[0] - rows, o_ref.shape[1]), F32)


def short_conv(p, conv_w, n_cols, col_block0=0):
    rows = p.shape[0]
    return pl.pallas_call(
        _short_conv_kernel,
        grid=(n_cols // LANES,),
        in_specs=[pl.BlockSpec((rows, LANES), lambda j: (0, j + col_block0)),
                  pl.BlockSpec((3, LANES), lambda j: (0, j))],
        out_specs=pl.BlockSpec((SEQ_PAD, LANES), lambda j: (0, j)),
        out_shape=jax.ShapeDtypeStruct((SEQ_PAD, n_cols), F32),
        compiler_params=_cparams(("parallel",)),
        name="short_conv",
    )(p, conv_w)


def _gate_kernel(g_ref, c_ref, z_ref, b_ref, o_ref):
    o_ref[...] = (g_ref[...] * (c_ref[...] + b_ref[...] * z_ref[...])).astype(o_ref.dtype)


def hyena_gate(uc, gate_block0, conv, z_src, z_block0, bias_row, rows, out_dtype, tr, tc=512):
    ch = conv.shape[1]
    return pl.pallas_call(
        _gate_kernel,
        grid=(rows // tr, ch // tc),
        in_specs=[pl.BlockSpec((tr, tc), lambda i, j: (i, j + gate_block0)),
                  pl.BlockSpec((tr, tc), lambda i, j: (i, j)),
                  pl.BlockSpec((tr, tc), lambda i, j: (i, j + z_block0)),
                  pl.BlockSpec((1, tc), lambda i, j: (0, j))],
        out_specs=pl.BlockSpec((tr, tc), lambda i, j: (i, j)),
        out_shape=jax.ShapeDtypeStruct((rows, ch), out_dtype),
        compiler_params=_cparams(("parallel", "parallel")),
        name="hyena_gate",
    )(uc, conv, z_src, bias_row)


def position_features(seq_len, n_bands):
    r = np.arange(FFT_N)
    pos = jnp.asarray(np.where(r < FFT_N // 2, r, FFT_N - r).astype(np.float32)[:, None])
    t = pos / (seq_len - 1)
    bands = jnp.linspace(1e-4, n_bands - 1, n_bands, dtype=F32)[None, :]
    ang = bands * (2.0 * math.pi) * pos / seq_len
    emb = jnp.concatenate([t, jnp.cos(ang), -jnp.sin(ang)], axis=-1)
    return jnp.pad(emb, ((0, 0), (0, LANES - emb.shape[1])))


def hyena_branch(p, col_block0, conv_w, w1, b1, w2, b2, freq, w3, bias, deltas, seq_len, n_bands, out_dtype):
    ch = deltas.shape[0]
    assert seq_len % FFT_N2 == 0 and seq_len // FFT_N2 <= SEQ_SLABS and 2 * seq_len - 1 <= FFT_N
    tables = fft_tables()
    emb = position_features(seq_len, n_bands)
    w1p = jnp.pad(w1, ((0, LANES - w1.shape[0]), (0, 0)))
    kern = filter_kernels(emb, w1p, b1, w2, b2, freq, w3, deltas, seq_len, tc=min(512, ch))
    kf = filter_spectrum(kern, tables[0], tables[2])
    uc = short_conv(p, conv_w, 3 * ch, col_block0)
    conv0 = long_conv(uc, 0, ch, kf, 0, tables)
    tcg = min(512, ch)
    gb = ch // tcg
    z1 = hyena_gate(uc, gb, conv0, uc, 0, bias[0:1], SEQ_PAD, F32, FILT_ROWS, tcg)
    conv1 = long_conv(z1, 0, ch, kf, 1, tables)
    return hyena_gate(uc, 2 * gb, conv1, z1, 0, bias[1:2], seq_len, out_dtype, 912, tcg)


def _mlstm_direction(d, chunk, q_ref, k_ref, v_ref, g_ref, bias_ref, tri_ref, o_ref, c_ref, n_ref, m_ref, seq_len):
    t_idx = lax.broadcasted_iota(jnp.int32, (ML_CHUNK, 1), 0)
    valid = t_idx < (seq_len - chunk * ML_CHUNK)
    g = g_ref[...] + bias_ref[...]
    li_all = jnp.where(valid, g[:, 16 * d:16 * d + 8], NEG_GATE)
    lf_all = jnp.where(valid, jax.nn.log_sigmoid(g[:, 16 * d + 8:16 * d + 16]), 0.0)
    cum_all = jnp.dot(tri_ref[d], lf_all, precision=HIGHEST, preferred_element_type=F32)
    tot_all = jnp.sum(lf_all, axis=0, keepdims=True)
    a_all = tot_all - cum_all + li_all
    amax_all = jnp.max(a_all, axis=0, keepdims=True)
    tt = lax.broadcasted_iota(jnp.int32, (ML_CHUNK, ML_CHUNK), 0)
    ss = lax.broadcasted_iota(jnp.int32, (ML_CHUNK, ML_CHUNK), 1)
    visible = (ss <= tt) if d == 0 else (ss >= tt)
    ones = jnp.ones((ML_CHUNK, 1), F32)
    scale = ML_HEAD_DIM ** -0.5
    for h in range(ML_HEADS):
        sl = slice(h * ML_HEAD_DIM, (h + 1) * ML_HEAD_DIM)
        q = jnp.where(valid, q_ref[:, sl], 0.0)
        k = jnp.where(valid, k_ref[:, sl], 0.0) * scale
        v = jnp.where(valid, v_ref[:, sl], 0.0)
        qb, kb, vb = q.astype(BF16), k.astype(BF16), v.astype(BF16)
        cum = cum_all[:, h:h + 1]
        li = li_all[:, h:h + 1]
        tot = tot_all[:, h:h + 1]
        st = d * ML_HEADS + h
        m_prev = m_ref[st]
        c_prev = c_ref[st]
        n_prev = n_ref[st]
        lhs = jnp.concatenate([cum, ones], axis=1)
        rhs = jnp.concatenate([ones, li - cum], axis=1)
        log_d = lax.dot_general(lhs, rhs, (((1,), (1,)), ((), ())), precision=HIGHEST,
                                preferred_element_type=F32)
        log_d = jnp.where(visible, log_d, -jnp.inf)
        log_inter = cum + m_prev
        m_t = jnp.maximum(log_inter, jnp.max(log_d, axis=1, keepdims=True))
        s = lax.dot_general(qb, kb, (((1,), (1,)), ((), ())), preferred_element_type=F32) * jnp.exp(log_d - m_t)
        w_inter = jnp.exp(log_inter - m_t)
        num = (jnp.dot(s.astype(BF16), vb, preferred_element_type=F32)
               + w_inter * jnp.dot(qb, c_prev.astype(BF16), preferred_element_type=F32))
        den = jnp.sum(s, axis=1, keepdims=True) + w_inter * jnp.sum(q * n_prev, axis=1, keepdims=True)
        o_ref[:, sl] = num / jnp.maximum(jnp.abs(den), jnp.exp(-m_t))
        m_new = jnp.maximum(tot + m_prev, amax_all[:, h:h + 1])
        decay = jnp.exp(tot + m_prev - m_new)
        kw = k * jnp.exp(a_all[:, h:h + 1] - m_new)
        c_ref[st] = decay * c_prev + lax.dot_general(kw.astype(BF16), vb, (((0,), (0,)), ((), ())),
                                                     preferred_element_type=F32)
        n_ref[st] = decay * n_prev + jnp.sum(kw, axis=0, keepdims=True)
        m_ref[st] = m_new


def _mlstm_kernel(qf, kf, vf, gf, qb, kb, vb, gb, bias_ref, tri_ref, of_ref, ob_ref, c_ref, n_ref, m_ref,
                  *, n_chunks, seq_len):
    i = pl.program_id(0)

    @pl.when(i == 0)
    def _():
        c_ref[...] = jnp.zeros(c_ref.shape, F32)
        n_ref[...] = jnp.zeros(n_ref.shape, F32)
        m_ref[...] = jnp.zeros(m_ref.shape, F32)

    _mlstm_direction(0, i, qf, kf, vf, gf, bias_ref, tri_ref, of_ref, c_ref, n_ref, m_ref, seq_len)
    _mlstm_direction(1, n_chunks - 1 - i, qb, kb, vb, gb, bias_ref, tri_ref, ob_ref, c_ref, n_ref, m_ref, seq_len)


def mlstm_scan(qkvo, p_gate, gate_bias):
    seq_len = qkvo.shape[0]
    w = ML_WIDTH
    n_chunks = pl.cdiv(seq_len, ML_CHUNK)
    tri = np.tril(np.ones((ML_CHUNK, ML_CHUNK), np.float32))
    tri = jnp.asarray(np.stack([tri, tri.T]))
    fw = lambda col: (lambda i: (i, col))
    bw = lambda col: (lambda i: (n_chunks - 1 - i, col))
    blk = lambda im: pl.BlockSpec((ML_CHUNK, w), im)
    gblk = lambda im: pl.BlockSpec((ML_CHUNK, 4 * ML_HEADS), im)
    return pl.pallas_call(
        functools.partial(_mlstm_kernel, n_chunks=n_chunks, seq_len=seq_len),
        grid=(n_chunks,),
        in_specs=[blk(fw(0)), blk(fw(1)), blk(fw(2)), gblk(fw(0)),
                  blk(bw(0)), blk(bw(1)), blk(bw(2)), gblk(bw(0)),
                  pl.BlockSpec((1, 4 * ML_HEADS), lambda i: (0, 0)),
                  pl.BlockSpec((2, ML_CHUNK, ML_CHUNK), lambda i: (0, 0, 0))],
        out_specs=[blk(fw(0)), blk(bw(0))],
        out_shape=[jax.ShapeDtypeStruct((seq_len, w), F32), jax.ShapeDtypeStruct((seq_len, w), F32)],
        scratch_shapes=[pltpu.VMEM((2 * ML_HEADS, ML_HEAD_DIM, ML_HEAD_DIM), F32),
                        pltpu.VMEM((2 * ML_HEADS, 1, ML_HEAD_DIM), F32),
                        pltpu.VMEM((2 * ML_HEADS, 1, 1), F32)],
        compiler_params=_cparams(("arbitrary",)),
        name="mlstm_scan",
    )(qkvo, qkvo, qkvo, p_gate, qkvo, qkvo, qkvo, p_gate, gate_bias.reshape(1, -1), tri)


def _mlstm_finish_kernel(hf_ref, hb_ref, o_ref, g_ref, y_ref):
    for h in range(ML_HEADS):
        sl = slice(h * ML_HEAD_DIM, (h + 1) * ML_HEAD_DIM)
        x = hf_ref[:, sl] + hb_ref[:, sl]
        hn = x * lax.rsqrt(jnp.mean(x * x, axis=-1, keepdims=True) + RMS_EPS) * g_ref[:, sl]
        y_ref[:, sl] = (jax.nn.sigmoid(o_ref[:, sl]) * hn).astype(y_ref.dtype)


def mlstm_finish(h_fw, h_bw, qkvo, head_norm, tm, out_dtype):
    seq_len, w = h_fw.shape
    row = lambda i: (i, 0)
    return pl.pallas_call(
        _mlstm_finish_kernel,
        grid=(seq_len // tm,),
        in_specs=[pl.BlockSpec((tm, w), row), pl.BlockSpec((tm, w), row),
                  pl.BlockSpec((tm, w), lambda i: (i, 3)), pl.BlockSpec((1, w), lambda i: (0, 0))],
        out_specs=pl.BlockSpec((tm, w), row),
        out_shape=jax.ShapeDtypeStruct((seq_len, w), out_dtype),
        compiler_params=_cparams(("parallel",)),
        name="mlstm_finish",
    )(h_fw, h_bw, qkvo, head_norm.reshape(1, w))


TOK_TILE = 240
WIN = 256
I32 = jnp.int32


def _router_kernel(h_ref, g_ref, w_ref, u_ref, a_ref):
    x = h_ref[...]
    u = (x * lax.rsqrt(jnp.mean(x * x, axis=-1, keepdims=True) + RMS_EPS) * g_ref[...]).astype(BF16)
    u_ref[...] = u
    logits = jnp.dot(u, w_ref[...].astype(BF16), preferred_element_type=F32)
    lane = lax.broadcasted_iota(I32, logits.shape, 1)
    logits = jnp.where(lane < N_EXPERTS, logits, -jnp.inf)
    z = jnp.exp(logits - jnp.max(logits, axis=-1, keepdims=True))
    a_ref[...] = z / jnp.sum(z, axis=-1, keepdims=True)


def router(h, g, w_router, tm):
    m, d = h.shape
    wp = jnp.pad(w_router, ((0, 0), (0, LANES - w_router.shape[1])))
    return pl.pallas_call(
        _router_kernel,
        grid=(m // tm,),
        in_specs=[pl.BlockSpec((tm, d), lambda i: (i, 0)), pl.BlockSpec((1, d), lambda i: (0, 0)),
                  pl.BlockSpec((d, LANES), lambda i: (0, 0))],
        out_specs=[pl.BlockSpec((tm, d), lambda i: (i, 0)), pl.BlockSpec((tm, LANES), lambda i: (i, 0))],
        out_shape=[jax.ShapeDtypeStruct((m, d), BF16), jax.ShapeDtypeStruct((m, LANES), F32)],
        compiler_params=_cparams(("parallel",)),
        name="router",
    )(h, g.reshape(1, d), wp)


def _topk_kernel(a_ref, tri_ref, eye_ref, posc_ref, posr_ref, start_ref, *, cap, n_tiles):
    bits = lax.bitcast_convert_type(a_ref[...], I32)

    def search(b, lo):
        cand = lo | (jnp.int32(1) << (30 - b))
        cnt = jnp.sum((bits >= cand).astype(I32), axis=0, keepdims=True)
        return jnp.where(cnt >= cap, cand, lo)

    thr = lax.fori_loop(0, 31, search, jnp.zeros((1, LANES), I32))
    n_gt = jnp.sum((bits > thr).astype(I32), axis=0, keepdims=True)
    need_eq = (cap - n_gt).astype(F32)
    tri = tri_ref[...]
    eye = eye_ref[...]

    def tile(t, carry):
        eq_before, sel_before = carry
        r0 = pl.multiple_of(t * TOK_TILE, 16)
        b = lax.bitcast_convert_type(a_ref[pl.ds(r0, TOK_TILE), :], I32)
        gt = b > thr
        eq = b == thr
        eq_rank = eq_before + jnp.dot(tri, eq.astype(BF16), preferred_element_type=F32)
        sel = gt | (eq & (eq_rank < need_eq))
        pos = sel_before + jnp.dot(tri, sel.astype(BF16), preferred_element_type=F32)
        posm = jnp.where(sel, pos, -1.0)
        posc_ref[pl.ds(r0, TOK_TILE), :] = posm
        posr_ref[t] = lax.dot_general(eye, posm, (((1,), (1,)), ((), ())), precision=HIGHEST,
                                      preferred_element_type=F32)
        start_ref[pl.ds(t, 1), :] = sel_before
        return (eq_before + jnp.sum(eq.astype(F32), axis=0, keepdims=True),
                sel_before + jnp.sum(sel.astype(F32), axis=0, keepdims=True))

    lax.fori_loop(0, n_tiles, tile, (jnp.zeros((1, LANES), F32), jnp.zeros((1, LANES), F32)))


def expert_choice_topk(aff_pad, cap):
    rows = aff_pad.shape[0]
    n_tiles = rows // TOK_TILE
    tri = jnp.asarray(np.tril(np.ones((TOK_TILE, TOK_TILE), np.float32), -1), BF16)
    eye = jnp.asarray(np.eye(LANES, dtype=np.float32))
    return pl.pallas_call(
        functools.partial(_topk_kernel, cap=cap, n_tiles=n_tiles),
        out_shape=[jax.ShapeDtypeStruct((rows, LANES), F32), jax.ShapeDtypeStruct((n_tiles, LANES, TOK_TILE), F32),
                   jax.ShapeDtypeStruct((n_tiles, LANES), F32)],
        compiler_params=pltpu.CompilerParams(vmem_limit_bytes=VMEM_LIMIT_BYTES),
        name="expert_choice_topk",
    )(aff_pad, tri, eye)


def _gather_kernel(start_ref, u_ref, pr_ref, o_ref, acc_ref, *, seq_len, cap_pad):
    e = pl.program_id(0)
    t = pl.program_id(1)

    @pl.when(t == 0)
    def _():
        acc_ref[...] = jnp.zeros(acc_ref.shape, F32)

    ws = pl.multiple_of((start_ref[e, t] // 16) * 16, 16)
    row = t * TOK_TILE + lax.broadcasted_iota(I32, (TOK_TILE, 1), 0)
    u = jnp.where(row < seq_len, u_ref[...], 0)
    j = (ws + lax.broadcasted_iota(I32, (WIN, 1), 0)).astype(F32)
    onehot = (j == pr_ref[0, 0]).astype(BF16)
    acc_ref[pl.ds(ws, WIN), :] += jnp.dot(onehot, u, preferred_element_type=F32)

    @pl.when(t == pl.num_programs(1) - 1)
    def _():
        o_ref[0] = acc_ref[0:cap_pad, :].astype(o_ref.dtype)


def gather_tokens(u, posm_row, tile_start, cap_pad):
    seq_len, d = u.shape
    n_tiles = posm_row.shape[1]
    grid_spec = pltpu.PrefetchScalarGridSpec(
        num_scalar_prefetch=1,
        grid=(N_EXPERTS, n_tiles),
        in_specs=[pl.BlockSpec((TOK_TILE, d), lambda e, t, s: (t, 0)),
                  pl.BlockSpec((1, 1, 1, TOK_TILE), lambda e, t, s: (e, t, 0, 0))],
        out_specs=pl.BlockSpec((1, cap_pad, d), lambda e, t, s: (e, 0, 0)),
        scratch_shapes=[pltpu.VMEM((cap_pad + WIN, d), F32)],
    )
    return pl.pallas_call(
        functools.partial(_gather_kernel, seq_len=seq_len, cap_pad=cap_pad),
        grid_spec=grid_spec,
        out_shape=jax.ShapeDtypeStruct((N_EXPERTS, cap_pad, d), BF16),
        compiler_params=_cparams(("parallel", "arbitrary")),
        name="gather_tokens",
    )(tile_start, u, posm_row)


def _combine_kernel(start_ref, h_ref, pc_ref, a_ref, g_ref, ye_ref, o_ref, buf_ref, sem_ref, *, cap_pad):
    t = pl.program_id(0)
    total = N_EXPERTS * cap_pad

    def window_start(e):
        ws = e * cap_pad + (start_ref[e, t] // 16) * 16
        return pl.multiple_of(jnp.minimum(ws, total - WIN), 16)

    def window_copy(e, slot):
        return pltpu.make_async_copy(ye_ref.at[pl.ds(window_start(e), WIN), :], buf_ref.at[slot], sem_ref.at[slot])

    window_copy(0, 0).start()
    acc = h_ref[...]
    posm = pc_ref[...]
    aff = a_ref[...]
    col = lax.broadcasted_iota(I32, (1, WIN), 1).astype(F32)
    for e in range(N_EXPERTS):
        slot = e % 2
        if e + 1 < N_EXPERTS:
            window_copy(e + 1, 1 - slot).start()
        window_copy(e, slot).wait()
        rel = posm[:, e:e + 1] + (e * cap_pad - window_start(e)).astype(F32)
        onehot = (rel == col).astype(BF16)
        y = jnp.dot(onehot, buf_ref[slot], preferred_element_type=F32)
        acc = acc + aff[:, e:e + 1] * y
    o_ref[...] = acc * lax.rsqrt(jnp.mean(acc * acc, axis=-1, keepdims=True) + RMS_EPS) * g_ref[...]


def moe_combine(h, posm_col, aff_pad, ye, tile_start, g):
    seq_len, d = h.shape
    n_tiles = posm_col.shape[0] // TOK_TILE
    cap_pad = ye.shape[1]
    grid_spec = pltpu.PrefetchScalarGridSpec(
        num_scalar_prefetch=1,
        grid=(n_tiles,),
        in_specs=[pl.BlockSpec((TOK_TILE, d), lambda t, s: (t, 0)),
                  pl.BlockSpec((TOK_TILE, LANES), lambda t, s: (t, 0)),
                  pl.BlockSpec((TOK_TILE, LANES), lambda t, s: (t, 0)),
                  pl.BlockSpec((1, d), lambda t, s: (0, 0)),
                  pl.BlockSpec(memory_space=pl.ANY)],
        out_specs=pl.BlockSpec((TOK_TILE, d), lambda t, s: (t, 0)),
        scratch_shapes=[pltpu.VMEM((2, WIN, d), BF16), pltpu.SemaphoreType.DMA((2,))],
    )
    return pl.pallas_call(
        functools.partial(_combine_kernel, cap_pad=cap_pad),
        grid_spec=grid_spec,
        out_shape=jax.ShapeDtypeStruct((seq_len, d), F32),
        compiler_params=_cparams(("arbitrary",)),
        name="moe_combine",
    )(tile_start, h, posm_col, aff_pad, g.reshape(1, d), ye.reshape(N_EXPERTS * cap_pad, d))


def kernel(x, meta_tokens, norm_mix, w_in, ml_gate_bias, ml_head_norm, hy_conv_w, hy_filt_w1, hy_filt_b1,
           hy_filt_w2, hy_filt_b2, hy_filt_freq, hy_filt_w3, hy_bias, w_branch_a, w_branch_b, w_out,
           norm_ffn, w_router, w_gate, w_up, w_down, norm_final):
    b_ = x.shape[0]
    assert b_ == 1
    h = jnp.concatenate([meta_tokens, x[0]], axis=0)
    seq_len = h.shape[0]
    tm = 912
    assert seq_len % tm == 0
    layer = 0

    w_in_l = w_in[layer]
    qkvo = norm_proj(h, norm_mix[layer], w_in_l, OFF_G, 0, tm, 1024)
    w_hm = w_in_l[:, OFF_HY:]
    p_hm = norm_proj(h, norm_mix[layer], w_hm, w_hm.shape[1], 0, tm, 1024)
    w_g = jnp.pad(w_in_l[:, OFF_G:OFF_HY], ((0, 0), (0, 128 - 4 * ML_HEADS)))
    p_g = norm_proj(h, norm_mix[layer], w_g, 128, 0, tm, 128)[:, :4 * ML_HEADS]

    h_fw, h_bw = mlstm_scan(qkvo, p_g, ml_gate_bias[layer])
    y_a = mlstm_finish(h_fw, h_bw, qkvo, ml_head_norm[layer], tm, BF16)
    max_decay = math.log(HY_DECAY_TARGET) / HY_FAST_DECAY_PCT
    min_decay = math.log(HY_DECAY_TARGET) / HY_SLOW_DECAY_PCT
    deltas = jnp.abs(jnp.linspace(min_decay, max_decay, HY_WIDTH, dtype=F32))
    y_b = hyena_branch(p_hm, 0, hy_conv_w[layer], hy_filt_w1[layer], hy_filt_b1[layer], hy_filt_w2[layer],
                       hy_filt_b2[layer], hy_filt_freq[layer], hy_filt_w3[layer], hy_bias[layer], deltas,
                       seq_len, HY_BANDS, BF16)

    merged = gated_dual_mm(y_a, y_b, p_hm, 3 * HY_WIDTH, w_branch_a[layer], w_branch_b[layer], tm, 512)
    h = mm_residual(merged, w_out[layer], h, tm, 1024)

    cap = EC_CAPACITY * seq_len // N_EXPERTS
    cap_pad = -(-cap // 16) * 16
    n_tiles = -(-seq_len // TOK_TILE)
    u, aff = router(h, norm_ffn[layer], w_router[layer], tm)
    aff_pad = jnp.pad(aff, ((0, n_tiles * TOK_TILE - seq_len), (0, 0)), constant_values=-1.0)
    posm_col, posm_row, start = expert_choice_topk(aff_pad, cap)
    tile_start = start[:, :N_EXPERTS].T.astype(jnp.int32)
    posm_row = jnp.transpose(posm_row[:, :N_EXPERTS, :], (1, 0, 2))[:, :, None, :]
    xe = gather_tokens(u, posm_row, tile_start, cap_pad)
    ye = expert_ffn(xe, w_gate[layer], w_up[layer], w_down[layer], 256)
    out = moe_combine(h, posm_col, aff_pad, ye, tile_start, norm_final)
    return out[None, N_META:]
```

```python
import functools
import math

import numpy as np
import jax
import jax.numpy as jnp
from jax import lax
from jax.experimental import pallas as pl
from jax.experimental.pallas import tpu as pltpu

D_MODEL = 2048
N_META = 16
ML_HEADS = 8
ML_HEAD_DIM = D_MODEL // ML_HEADS
ML_WIDTH = ML_HEADS * ML_HEAD_DIM
ML_CHUNK = 256
HY_WIDTH = D_MODEL
HY_ORDER = 2
HY_BANDS = 16
HY_FAST_DECAY_PCT = 0.3
HY_SLOW_DECAY_PCT = 1.5
HY_DECAY_TARGET = 1e-2
N_EXPERTS = 16
EC_CAPACITY = 2
D_EXPERT = 5504
RMS_EPS = 1e-6
NEG_GATE = -1e9

OFF_G = 4 * ML_WIDTH
OFF_HY = OFF_G + 4 * ML_HEADS
OFF_MERGE = OFF_HY + 3 * HY_WIDTH

VMEM_LIMIT_BYTES = 56 * 1024 * 1024

F32 = jnp.float32
BF16 = jnp.bfloat16


def _cparams(sem):
    return pltpu.CompilerParams(dimension_semantics=sem, vmem_limit_bytes=VMEM_LIMIT_BYTES)


def _norm_proj_kernel(x_ref, g_ref, w_ref, o_ref, u_ref):
    @pl.when(pl.program_id(1) == 0)
    def _():
        x = x_ref[...]
        y = x * lax.rsqrt(jnp.mean(x * x, axis=-1, keepdims=True) + RMS_EPS)
        u_ref[...] = (y * g_ref[...]).astype(BF16)

    o_ref[...] = jnp.dot(u_ref[...], w_ref[...].astype(BF16), preferred_element_type=F32)


def norm_proj(x, g, w, n_cols, col_block0, tm, tn):
    m, d = x.shape
    grid = (m // tm, n_cols // tn)
    return pl.pallas_call(
        _norm_proj_kernel,
        grid=grid,
        in_specs=[
            pl.BlockSpec((tm, d), lambda i, j: (i, 0)),
            pl.BlockSpec((1, d), lambda i, j: (0, 0)),
            pl.BlockSpec((d, tn), lambda i, j: (0, j + col_block0)),
        ],
        out_specs=pl.BlockSpec((tm, tn), lambda i, j: (i, j)),
        out_shape=jax.ShapeDtypeStruct((m, n_cols), F32),
        scratch_shapes=[pltpu.VMEM((tm, d), BF16)],
        compiler_params=_cparams(("parallel", "arbitrary")),
        name="norm_proj",
    )(x, g.reshape(1, d), w)


def _gated_dual_mm_kernel(ya_ref, yb_ref, pa_ref, pb_ref, wa_ref, wb_ref, o_ref):
    za = jnp.dot(ya_ref[...], wa_ref[...].astype(BF16), preferred_element_type=F32)
    zb = jnp.dot(yb_ref[...], wb_ref[...].astype(BF16), preferred_element_type=F32)
    merged = jax.nn.sigmoid(pa_ref[...]) * za + jax.nn.sigmoid(pb_ref[...]) * zb
    o_ref[...] = merged.astype(o_ref.dtype)


def gated_dual_mm(ya, yb, p_merge, gate_col0, wa, wb, tm, tn):
    m, d = ya.shape
    nb = d // tn
    g0 = gate_col0 // tn
    return pl.pallas_call(
        _gated_dual_mm_kernel,
        grid=(m // tm, nb),
        in_specs=[pl.BlockSpec((tm, d), lambda i, j: (i, 0)),
                  pl.BlockSpec((tm, d), lambda i, j: (i, 0)),
                  pl.BlockSpec((tm, tn), lambda i, j: (i, j + g0)),
                  pl.BlockSpec((tm, tn), lambda i, j: (i, j + g0 + nb)),
                  pl.BlockSpec((d, tn), lambda i, j: (0, j)),
                  pl.BlockSpec((d, tn), lambda i, j: (0, j))],
        out_specs=pl.BlockSpec((tm, tn), lambda i, j: (i, j)),
        out_shape=jax.ShapeDtypeStruct((m, d), BF16),
        compiler_params=_cparams(("parallel", "arbitrary")),
        name="gated_dual_mm",
    )(ya, yb, p_merge, p_merge, wa, wb)


def _mm_res_kernel(a_ref, b_ref, r_ref, o_ref):
    o_ref[...] = r_ref[...] + jnp.dot(a_ref[...], b_ref[...].astype(BF16), preferred_element_type=F32)


def mm_residual(a, b, res, tm, tn):
    m, k = a.shape
    _, n = b.shape
    return pl.pallas_call(
        _mm_res_kernel,
        grid=(m // tm, n // tn),
        in_specs=[pl.BlockSpec((tm, k), lambda i, j: (i, 0)),
                  pl.BlockSpec((k, tn), lambda i, j: (0, j)),
                  pl.BlockSpec((tm, tn), lambda i, j: (i, j))],
        out_specs=pl.BlockSpec((tm, tn), lambda i, j: (i, j)),
        out_shape=jax.ShapeDtypeStruct((m, n), F32),
        compiler_params=_cparams(("parallel", "arbitrary")),
        name="mm_residual",
    )(a, b, res)


def _ffn_kernel(xe_ref, wg_ref, wu_ref, wd_ref, o_ref, acc_ref, *, tf, f_total):
    f = pl.program_id(1)

    @pl.when(f == 0)
    def _():
        acc_ref[...] = jnp.zeros(acc_ref.shape, F32)

    x = xe_ref[0]
    a = jnp.dot(x, wg_ref[0].astype(BF16), preferred_element_type=F32)
    b = jnp.dot(x, wu_ref[0].astype(BF16), preferred_element_type=F32)
    hid = (a * jax.nn.sigmoid(a)) * b
    col_ok = (f * tf + lax.broadcasted_iota(jnp.int32, (1, tf), 1)) < f_total
    row_ok = (f * tf + lax.broadcasted_iota(jnp.int32, (tf, 1), 0)) < f_total
    hid = jnp.where(col_ok, hid, 0.0).astype(BF16)
    wd = jnp.where(row_ok, wd_ref[0], 0.0).astype(BF16)
    acc_ref[...] += jnp.dot(hid, wd, preferred_element_type=F32)

    @pl.when(f == pl.num_programs(1) - 1)
    def _():
        o_ref[0] = acc_ref[...].astype(o_ref.dtype)


def expert_ffn(xe, wg, wu, wd, tf):
    e, c, d = xe.shape
    f_total = wg.shape[-1]
    nf = pl.cdiv(f_total, tf)
    return pl.pallas_call(
        functools.partial(_ffn_kernel, tf=tf, f_total=f_total),
        grid=(e, nf),
        in_specs=[pl.BlockSpec((1, c, d), lambda i, f: (i, 0, 0)),
                  pl.BlockSpec((1, d, tf), lambda i, f: (i, 0, f)),
                  pl.BlockSpec((1, d, tf), lambda i, f: (i, 0, f)),
                  pl.BlockSpec((1, tf, d), lambda i, f: (i, f, 0))],
        out_specs=pl.BlockSpec((1, c, d), lambda i, f: (i, 0, 0)),
        out_shape=jax.ShapeDtypeStruct((e, c, d), BF16),
        scratch_shapes=[pltpu.VMEM((c, d), F32)],
        compiler_params=_cparams(("parallel", "arbitrary")),
        name="expert_ffn",
    )(xe, wg, wu, wd)


FFT_N1 = 256
FFT_N2 = 72
FFT_N = FFT_N1 * FFT_N2
SEQ_SLABS = 128
SEQ_PAD = SEQ_SLABS * FFT_N2
LANES = 128
K1_HALF = 144
K1_CHUNK = 48
BUF_ROWS = K1_HALF * FFT_N2
K1_UNROLL = 8
N2_UNROLL = 3
HY_ORDERS = 2
FILT_ROWS = 1152
HIGHEST = lax.Precision.HIGHEST


def fft_tables():
    n1 = np.arange(FFT_N1)
    ang1 = 2.0 * np.pi * np.outer(n1, n1) / FFT_N1
    c1, s1 = np.cos(ang1), np.sin(ang1)
    f1_fwd = np.concatenate([c1[:K1_HALF], -s1[:K1_HALF]], axis=0)
    wgt = np.where(np.arange(K1_HALF) > FFT_N1 // 2, 0.0, 2.0)
    wgt[0] = 1.0
    wgt[FFT_N1 // 2] = 1.0
    f1_inv = np.concatenate([c1[:, :K1_HALF] * wgt, -s1[:, :K1_HALF] * wgt], axis=1) / FFT_N
    k2 = np.arange(FFT_N2)
    n2 = np.arange(FFT_N2)
    k1 = np.arange(K1_HALF)
    ang = 2.0 * np.pi * (n2[None, None, :] * (k1[:, None, None] + FFT_N1 * k2[None, :, None])) / FFT_N
    g = np.concatenate([np.cos(ang), -np.sin(ang)], axis=2)
    angt = np.transpose(ang, (0, 2, 1))
    h = np.concatenate([np.cos(angt), -np.sin(angt)], axis=2)
    return (jnp.asarray(f1_fwd, F32), jnp.asarray(f1_inv, F32), jnp.asarray(g, F32), jnp.asarray(h, F32))


def _stage1_fwd(x_ref, f1_ref, buf_ref, n_rows):
    f1 = f1_ref[:, 0:n_rows].astype(BF16)

    def body(j, carry):
        na = 2 * j
        sa = x_ref[pl.ds(na, n_rows, stride=FFT_N2), :]
        sb = x_ref[pl.ds(na + 1, n_rows, stride=FFT_N2), :]
        s = jnp.concatenate([sa, sb], axis=1).astype(BF16)
        u = jnp.dot(f1, s, preferred_element_type=F32)
        buf_ref[0, pl.ds(na, K1_HALF, stride=FFT_N2), :] = u[0:K1_HALF, 0:LANES]
        buf_ref[0, pl.ds(na + 1, K1_HALF, stride=FFT_N2), :] = u[0:K1_HALF, LANES:]
        buf_ref[1, pl.ds(na, K1_HALF, stride=FFT_N2), :] = u[K1_HALF:, 0:LANES]
        buf_ref[1, pl.ds(na + 1, K1_HALF, stride=FFT_N2), :] = u[K1_HALF:, LANES:]
        return carry

    lax.fori_loop(0, FFT_N2 // 2, body, 0, unroll=N2_UNROLL)


def _stage2_fwd(buf_ref, g_ref, row0, kk):
    ar = buf_ref[0, pl.ds(row0, FFT_N2), :]
    ai = buf_ref[1, pl.ds(row0, FFT_N2), :]
    t = jnp.concatenate([jnp.concatenate([ar, ai], axis=1),
                         jnp.concatenate([-ai, ar], axis=1)], axis=0).astype(BF16)
    x = jnp.dot(g_ref[kk].astype(BF16), t, preferred_element_type=F32)
    return x[:, 0:LANES], x[:, LANES:]


def _spectrum_kernel(k_ref, f1_ref, g_ref, o_ref, buf_ref):
    c = pl.program_id(2)

    @pl.when(c == 0)
    def _():
        _stage1_fwd(k_ref, f1_ref, buf_ref, FFT_N1)

    def body(kk, carry):
        row0 = pl.multiple_of((c * K1_CHUNK + kk) * FFT_N2, 8)
        xr, xi = _stage2_fwd(buf_ref, g_ref, row0, kk)
        o0 = pl.multiple_of(kk * FFT_N2, 8)
        o_ref[0, pl.ds(o0, FFT_N2), :] = xr
        o_ref[1, pl.ds(o0, FFT_N2), :] = xi
        return carry

    lax.fori_loop(0, K1_CHUNK, body, 0, unroll=K1_UNROLL)


def filter_spectrum(kern, f1_fwd, g):
    no, n, ch = kern.shape
    nchunk = K1_HALF // K1_CHUNK
    return pl.pallas_call(
        _spectrum_kernel,
        grid=(no, ch // LANES, nchunk),
        in_specs=[pl.BlockSpec((None, n, LANES), lambda o, b, c: (o, 0, b)),
                  pl.BlockSpec((2 * K1_HALF, FFT_N1), lambda o, b, c: (0, 0)),
                  pl.BlockSpec((K1_CHUNK, FFT_N2, 2 * FFT_N2), lambda o, b, c: (c, 0, 0))],
        out_specs=pl.BlockSpec((None, 2, K1_CHUNK * FFT_N2, LANES), lambda o, b, c: (o, 0, c, b)),
        out_shape=jax.ShapeDtypeStruct((no, 2, BUF_ROWS, ch), F32),
        scratch_shapes=[pltpu.VMEM((2, BUF_ROWS, LANES), F32)],
        compiler_params=_cparams(("parallel", "parallel", "arbitrary")),
        name="filter_spectrum",
    )(kern, f1_fwd, g)


def _conv_kernel(z_ref, kf_ref, f1_ref, f1i_ref, g_ref, h_ref, o_ref, buf_ref):
    c = pl.program_id(1)

    @pl.when(c == 0)
    def _():
        _stage1_fwd(z_ref, f1_ref, buf_ref, SEQ_SLABS)

    def body(kk, carry):
        row0 = pl.multiple_of((c * K1_CHUNK + kk) * FFT_N2, 8)
        xr, xi = _stage2_fwd(buf_ref, g_ref, row0, kk)
        k0 = pl.multiple_of(kk * FFT_N2, 8)
        kr = kf_ref[0, pl.ds(k0, FFT_N2), :]
        ki = kf_ref[1, pl.ds(k0, FFT_N2), :]
        pr = xr * kr - xi * ki
        pi = xr * ki + xi * kr
        t = jnp.concatenate([jnp.concatenate([pr, pi], axis=1),
                             jnp.concatenate([pi, -pr], axis=1)], axis=0).astype(BF16)
        q = jnp.dot(h_ref[kk].astype(BF16), t, preferred_element_type=F32)
        buf_ref[0, pl.ds(row0, FFT_N2), :] = q[:, 0:LANES]
        buf_ref[1, pl.ds(row0, FFT_N2), :] = q[:, LANES:]
        return carry

    lax.fori_loop(0, K1_CHUNK, body, 0, unroll=K1_UNROLL)

    @pl.when(c == pl.num_programs(1) - 1)
    def _():
        f1i = f1i_ref[0:SEQ_SLABS, :].astype(BF16)

        def inv_body(j, carry):
            na = 2 * j
            sra = buf_ref[0, pl.ds(na, K1_HALF, stride=FFT_N2), :]
            srb = buf_ref[0, pl.ds(na + 1, K1_HALF, stride=FFT_N2), :]
            sia = buf_ref[1, pl.ds(na, K1_HALF, stride=FFT_N2), :]
            sib = buf_ref[1, pl.ds(na + 1, K1_HALF, stride=FFT_N2), :]
            s = jnp.concatenate([jnp.concatenate([sra, srb], axis=1),
                                 jnp.concatenate([sia, sib], axis=1)], axis=0).astype(BF16)
            y = jnp.dot(f1i, s, preferred_element_type=F32)
            o_ref[pl.ds(na, SEQ_SLABS, stride=FFT_N2), :] = y[:, 0:LANES]
            o_ref[pl.ds(na + 1, SEQ_SLABS, stride=FFT_N2), :] = y[:, LANES:]
            return carry

        lax.fori_loop(0, FFT_N2 // 2, inv_body, 0, unroll=N2_UNROLL)


def long_conv(z_src, z_block0, ch, kf, order, tables):
    f1_fwd, f1_inv, g, h = tables
    rows = z_src.shape[0]
    assert rows == SEQ_PAD
    nchunk = K1_HALF // K1_CHUNK
    return pl.pallas_call(
        _conv_kernel,
        grid=(ch // LANES, nchunk),
        in_specs=[pl.BlockSpec((rows, LANES), lambda b, c: (0, b + z_block0)),
                  pl.BlockSpec((None, 2, K1_CHUNK * FFT_N2, LANES), lambda b, c: (order, 0, c, b)),
                  pl.BlockSpec((2 * K1_HALF, FFT_N1), lambda b, c: (0, 0)),
                  pl.BlockSpec((FFT_N1, 2 * K1_HALF), lambda b, c: (0, 0)),
                  pl.BlockSpec((K1_CHUNK, FFT_N2, 2 * FFT_N2), lambda b, c: (c, 0, 0)),
                  pl.BlockSpec((K1_CHUNK, FFT_N2, 2 * FFT_N2), lambda b, c: (c, 0, 0))],
        out_specs=pl.BlockSpec((rows, LANES), lambda b, c: (0, b)),
        out_shape=jax.ShapeDtypeStruct((rows, ch), F32),
        scratch_shapes=[pltpu.VMEM((2, BUF_ROWS, LANES), F32)],
        compiler_params=_cparams(("parallel", "arbitrary")),
        name="long_conv",
    )(z_src, kf, f1_fwd, f1_inv, g, h)


def _filter_kernel(emb_ref, w1_ref, b1_ref, w2_ref, b2_ref, fr_ref, w3_ref, dl_ref, o_ref, h_ref, *, seq_len):
    i = pl.program_id(0)

    @pl.when((pl.program_id(1) == 0) & (pl.program_id(2) == 0))
    def _():
        fr = fr_ref[...]
        h1 = jnp.sin(fr * (jnp.dot(emb_ref[...], w1_ref[...], precision=HIGHEST, preferred_element_type=F32)
                           + b1_ref[...]))
        h_ref[...] = jnp.sin(fr * (jnp.dot(h1, w2_ref[...], precision=HIGHEST, preferred_element_type=F32)
                                   + b2_ref[...]))

    h2 = h_ref[...]
    h_hi = h2.astype(BF16)
    h_lo = (h2 - h_hi.astype(F32)).astype(BF16)
    w3 = w3_ref[...]
    w_hi = w3.astype(BF16)
    w_lo = (w3 - w_hi.astype(F32)).astype(BF16)
    f = (jnp.dot(h_hi, w_hi, preferred_element_type=F32)
         + (jnp.dot(h_hi, w_lo, preferred_element_type=F32) + jnp.dot(h_lo, w_hi, preferred_element_type=F32)))
    t = emb_ref[:, 0:1]
    f = f * jnp.exp(-t * dl_ref[...])
    r = i * FILT_ROWS + lax.broadcasted_iota(jnp.int32, (FILT_ROWS, 1), 0)
    fwd = i < (FFT_N // 2) // FILT_ROWS
    lo = jnp.where(fwd, -1, FFT_N - seq_len)
    hi = jnp.where(fwd, seq_len, FFT_N)
    valid = (r > lo) & (r < hi)
    o_ref[0] = jnp.where(valid, f, 0.0)


def filter_kernels(emb, w1p, b1, w2, b2, freq, w3, deltas, seq_len, tc=512):
    ch = deltas.shape[0]
    hid = w2.shape[0]
    ncb = ch // tc
    half = (FFT_N // 2) // FILT_ROWS
    full = lambda i, o, j: (0, 0)
    return pl.pallas_call(
        functools.partial(_filter_kernel, seq_len=seq_len),
        grid=(FFT_N // FILT_ROWS, HY_ORDERS, ncb),
        in_specs=[pl.BlockSpec((FILT_ROWS, emb.shape[1]), lambda i, o, j: (i, 0)),
                  pl.BlockSpec(w1p.shape, full), pl.BlockSpec((1, hid), full),
                  pl.BlockSpec((hid, hid), full), pl.BlockSpec((1, hid), full), pl.BlockSpec((1, hid), full),
                  pl.BlockSpec((hid, tc), lambda i, o, j: (0, (2 * o + i // half) * ncb + j)),
                  pl.BlockSpec((1, tc), lambda i, o, j: (0, j))],
        out_specs=pl.BlockSpec((1, FILT_ROWS, tc), lambda i, o, j: (o, i, j)),
        out_shape=jax.ShapeDtypeStruct((HY_ORDERS, FFT_N, ch), F32),
        scratch_shapes=[pltpu.VMEM((FILT_ROWS, hid), F32)],
        compiler_params=_cparams(("parallel", "arbitrary", "arbitrary")),
        name="hyena_filters",
    )(emb, w1p, b1.reshape(1, hid), w2, b2.reshape(1, hid), freq.reshape(1, hid), w3, deltas.reshape(1, ch))


def _short_conv_kernel(u_ref, w_ref, o_ref):
    u = u_ref[...]
    rows = u.shape[0]
    r = lax.broadcasted_iota(jnp.int32, (rows, 1), 0)
    prev = jnp.where(r == 0, 0.0, pltpu.roll(u, 1, axis=0))
    nxt = jnp.where(r == rows - 1, 0.0, pltpu.roll(u, rows - 1, axis=0))
    w = w_ref[...]
    o_ref[0:rows, :] = w[0:1, :] * prev + w[1:2, :] * u + w[2:3, :] * nxt
    o_ref[rows:, :] = jnp.zeros((o_ref.shape[0] - rows, o_ref.shape[1]), F32)


def short_conv(p, conv_w, n_cols, col_block0=0):
    rows = p.shape[0]
    return pl.pallas_call(
        _short_conv_kernel,
        grid=(n_cols // LANES,),
        in_specs=[pl.BlockSpec((rows, LANES), lambda j: (0, j + col_block0)),
                  pl.BlockSpec((3, LANES), lambda j: (0, j))],
        out_specs=pl.BlockSpec((SEQ_PAD, LANES), lambda j: (0, j)),
        out_shape=jax.ShapeDtypeStruct((SEQ_PAD, n_cols), F32),
        compiler_params=_cparams(("parallel",)),
        name="short_conv",
    )(p, conv_w)


def _gate_kernel(g_ref, c_ref, z_ref, b_ref, o_ref):
    o_ref[...] = (g_ref[...] * (c_ref[...] + b_ref[...] * z_ref[...])).astype(o_ref.dtype)


def hyena_gate(uc, gate_block0, conv, z_src, z_block0, bias_row, rows, out_dtype, tr, tc=512):
    ch = conv.shape[1]
    return pl.pallas_call(
        _gate_kernel,
        grid=(rows // tr, ch // tc),
        in_specs=[pl.BlockSpec((tr, tc), lambda i, j: (i, j + gate_block0)),
                  pl.BlockSpec((tr, tc), lambda i, j: (i, j)),
                  pl.BlockSpec((tr, tc), lambda i, j: (i, j + z_block0)),
                  pl.BlockSpec((1, tc), lambda i, j: (0, j))],
        out_specs=pl.BlockSpec((tr, tc), lambda i, j: (i, j)),
        out_shape=jax.ShapeDtypeStruct((rows, ch), out_dtype),
        compiler_params=_cparams(("parallel", "parallel")),
        name="hyena_gate",
    )(uc, conv, z_src, bias_row)


def position_features(seq_len, n_bands):
    r = np.arange(FFT_N)
    pos = jnp.asarray(np.where(r < FFT_N // 2, r, FFT_N - r).astype(np.float32)[:, None])
    t = pos / (seq_len - 1)
    bands = jnp.linspace(1e-4, n_bands - 1, n_bands, dtype=F32)[None, :]
    ang = bands * (2.0 * math.pi) * pos / seq_len
    emb = jnp.concatenate([t, jnp.cos(ang), -jnp.sin(ang)], axis=-1)
    return jnp.pad(emb, ((0, 0), (0, LANES - emb.shape[1])))


def hyena_branch(p, col_block0, conv_w, w1, b1, w2, b2, freq, w3, bias, deltas, seq_len, n_bands, out_dtype):
    ch = deltas.shape[0]
    assert seq_len % FFT_N2 == 0 and seq_len // FFT_N2 <= SEQ_SLABS and 2 * seq_len - 1 <= FFT_N
    tables = fft_tables()
    emb = position_features(seq_len, n_bands)
    w1p = jnp.pad(w1, ((0, LANES - w1.shape[0]), (0, 0)))
    kern = filter_kernels(emb, w1p, b1, w2, b2, freq, w3, deltas, seq_len, tc=min(512, ch))
    kf = filter_spectrum(kern, tables[0], tables[2])
    uc = short_conv(p, conv_w, 3 * ch, col_block0)
    conv0 = long_conv(uc, 0, ch, kf, 0, tables)
    tcg = min(512, ch)
    gb = ch // tcg
    z1 = hyena_gate(uc, gb, conv0, uc, 0, bias[0:1], SEQ_PAD, F32, FILT_ROWS, tcg)
    conv1 = long_conv(z1, 0, ch, kf, 1, tables)
    return hyena_gate(uc, 2 * gb, conv1, z1, 0, bias[1:2], seq_len, out_dtype, 912, tcg)


def _mlstm_direction(d, chunk, q_ref, k_ref, v_ref, g_ref, bias_ref, tri_ref, o_ref, c_ref, n_ref, m_ref, seq_len):
    t_idx = lax.broadcasted_iota(jnp.int32, (ML_CHUNK, 1), 0)
    valid = t_idx < (seq_len - chunk * ML_CHUNK)
    g = g_ref[...] + bias_ref[...]
    li_all = jnp.where(valid, g[:, 16 * d:16 * d + 8], NEG_GATE)
    lf_all = jnp.where(valid, jax.nn.log_sigmoid(g[:, 16 * d + 8:16 * d + 16]), 0.0)
    cum_all = jnp.dot(tri_ref[d], lf_all, precision=HIGHEST, preferred_element_type=F32)
    tot_all = jnp.sum(lf_all, axis=0, keepdims=True)
    a_all = tot_all - cum_all + li_all
    amax_all = jnp.max(a_all, axis=0, keepdims=True)
    tt = lax.broadcasted_iota(jnp.int32, (ML_CHUNK, ML_CHUNK), 0)
    ss = lax.broadcasted_iota(jnp.int32, (ML_CHUNK, ML_CHUNK), 1)
    visible = (ss <= tt) if d == 0 else (ss >= tt)
    ones = jnp.ones((ML_CHUNK, 1), F32)
    scale = ML_HEAD_DIM ** -0.5
    for h in range(ML_HEADS):
        sl = slice(h * ML_HEAD_DIM, (h + 1) * ML_HEAD_DIM)
        q = jnp.where(valid, q_ref[:, sl], 0.0)
        k = jnp.where(valid, k_ref[:, sl], 0.0) * scale
        v = jnp.where(valid, v_ref[:, sl], 0.0)
        qb, kb, vb = q.astype(BF16), k.astype(BF16), v.astype(BF16)
        cum = cum_all[:, h:h + 1]
        li = li_all[:, h:h + 1]
        tot = tot_all[:, h:h + 1]
        st = d * ML_HEADS + h
        m_prev = m_ref[st]
        c_prev = c_ref[st]
        n_prev = n_ref[st]
        lhs = jnp.concatenate([cum, ones], axis=1)
        rhs = jnp.concatenate([ones, li - cum], axis=1)
        log_d = lax.dot_general(lhs, rhs, (((1,), (1,)), ((), ())), precision=HIGHEST,
                                preferred_element_type=F32)
        log_d = jnp.where(visible, log_d, -jnp.inf)
        log_inter = cum + m_prev
        m_t = jnp.maximum(log_inter, jnp.max(log_d, axis=1, keepdims=True))
        s = lax.dot_general(qb, kb, (((1,), (1,)), ((), ())), preferred_element_type=F32) * jnp.exp(log_d - m_t)
        w_inter = jnp.exp(log_inter - m_t)
        num = (jnp.dot(s.astype(BF16), vb, preferred_element_type=F32)
               + w_inter * jnp.dot(qb, c_prev.astype(BF16), preferred_element_type=F32))
        den = jnp.sum(s, axis=1, keepdims=True) + w_inter * jnp.sum(q * n_prev, axis=1, keepdims=True)
        o_ref[:, sl] = num / jnp.maximum(jnp.abs(den), jnp.exp(-m_t))
        m_new = jnp.maximum(tot + m_prev, amax_all[:, h:h + 1])
        decay = jnp.exp(tot + m_prev - m_new)
        kw = k * jnp.exp(a_all[:, h:h + 1] - m_new)
        c_ref[st] = decay * c_prev + lax.dot_general(kw.astype(BF16), vb, (((0,), (0,)), ((), ())),
                                                     preferred_element_type=F32)
        n_ref[st] = decay * n_prev + jnp.sum(kw, axis=0, keepdims=True)
        m_ref[st] = m_new


def _mlstm_kernel(qf, kf, vf, gf, qb, kb, vb, gb, bias_ref, tri_ref, of_ref, ob_ref, c_ref, n_ref, m_ref,
                  *, n_chunks, seq_len):
    i = pl.program_id(0)

    @pl.when(i == 0)
    def _():
        c_ref[...] = jnp.zeros(c_ref.shape, F32)
        n_ref[...] = jnp.zeros(n_ref.shape, F32)
        m_ref[...] = jnp.zeros(m_ref.shape, F32)

    _mlstm_direction(0, i, qf, kf, vf, gf, bias_ref, tri_ref, of_ref, c_ref, n_ref, m_ref, seq_len)
    _mlstm_direction(1, n_chunks - 1 - i, qb, kb, vb, gb, bias_ref, tri_ref, ob_ref, c_ref, n_ref, m_ref, seq_len)


def mlstm_scan(qkvo, p_gate, gate_bias):
    seq_len = qkvo.shape[0]
    w = ML_WIDTH
    n_chunks = pl.cdiv(seq_len, ML_CHUNK)
    tri = np.tril(np.ones((ML_CHUNK, ML_CHUNK), np.float32))
    tri = jnp.asarray(np.stack([tri, tri.T]))
    fw = lambda col: (lambda i: (i, col))
    bw = lambda col: (lambda i: (n_chunks - 1 - i, col))
    blk = lambda im: pl.BlockSpec((ML_CHUNK, w), im)
    gblk = lambda im: pl.BlockSpec((ML_CHUNK, 4 * ML_HEADS), im)
    return pl.pallas_call(
        functools.partial(_mlstm_kernel, n_chunks=n_chunks, seq_len=seq_len),
        grid=(n_chunks,),
        in_specs=[blk(fw(0)), blk(fw(1)), blk(fw(2)), gblk(fw(0)),
                  blk(bw(0)), blk(bw(1)), blk(bw(2)), gblk(bw(0)),
                  pl.BlockSpec((1, 4 * ML_HEADS), lambda i: (0, 0)),
                  pl.BlockSpec((2, ML_CHUNK, ML_CHUNK), lambda i: (0, 0, 0))],
        out_specs=[blk(fw(0)), blk(bw(0))],
        out_shape=[jax.ShapeDtypeStruct((seq_len, w), F32), jax.ShapeDtypeStruct((seq_len, w), F32)],
        scratch_shapes=[pltpu.VMEM((2 * ML_HEADS, ML_HEAD_DIM, ML_HEAD_DIM), F32),
                        pltpu.VMEM((2 * ML_HEADS, 1, ML_HEAD_DIM), F32),
                        pltpu.VMEM((2 * ML_HEADS, 1, 1), F32)],
        compiler_params=_cparams(("arbitrary",)),
        name="mlstm_scan",
    )(qkvo, qkvo, qkvo, p_gate, qkvo, qkvo, qkvo, p_gate, gate_bias.reshape(1, -1), tri)


def _mlstm_finish_kernel(hf_ref, hb_ref, o_ref, g_ref, y_ref):
    for h in range(ML_HEADS):
        sl = slice(h * ML_HEAD_DIM, (h + 1) * ML_HEAD_DIM)
        x = hf_ref[:, sl] + hb_ref[:, sl]
        hn = x * lax.rsqrt(jnp.mean(x * x, axis=-1, keepdims=True) + RMS_EPS) * g_ref[:, sl]
        y_ref[:, sl] = (jax.nn.sigmoid(o_ref[:, sl]) * hn).astype(y_ref.dtype)


def mlstm_finish(h_fw, h_bw, qkvo, head_norm, tm, out_dtype):
    seq_len, w = h_fw.shape
    row = lambda i: (i, 0)
    return pl.pallas_call(
        _mlstm_finish_kernel,
        grid=(seq_len // tm,),
        in_specs=[pl.BlockSpec((tm, w), row), pl.BlockSpec((tm, w), row),
                  pl.BlockSpec((tm, w), lambda i: (i, 3)), pl.BlockSpec((1, w), lambda i: (0, 0))],
        out_specs=pl.BlockSpec((tm, w), row),
        out_shape=jax.ShapeDtypeStruct((seq_len, w), out_dtype),
        compiler_params=_cparams(("parallel",)),
        name="mlstm_finish",
    )(h_fw, h_bw, qkvo, head_norm.reshape(1, w))


TOK_TILE = 240
WIN = 256
I32 = jnp.int32
COMBINE_BUFS = 4
GATHER_EXPERTS = 2


def _router_kernel(h_ref, g_ref, w_ref, u_ref, a_ref):
    x = h_ref[...]
    u = (x * lax.rsqrt(jnp.mean(x * x, axis=-1, keepdims=True) + RMS_EPS) * g_ref[...]).astype(BF16)
    u_ref[...] = u
    logits = jnp.dot(u, w_ref[...].astype(BF16), preferred_element_type=F32)
    lane = lax.broadcasted_iota(I32, logits.shape, 1)
    logits = jnp.where(lane < N_EXPERTS, logits, -jnp.inf)
    z = jnp.exp(logits - jnp.max(logits, axis=-1, keepdims=True))
    a_ref[...] = z / jnp.sum(z, axis=-1, keepdims=True)


def router(h, g, w_router, tm):
    m, d = h.shape
    wp = jnp.pad(w_router, ((0, 0), (0, LANES - w_router.shape[1])))
    return pl.pallas_call(
        _router_kernel,
        grid=(m // tm,),
        in_specs=[pl.BlockSpec((tm, d), lambda i: (i, 0)), pl.BlockSpec((1, d), lambda i: (0, 0)),
                  pl.BlockSpec((d, LANES), lambda i: (0, 0))],
        out_specs=[pl.BlockSpec((tm, d), lambda i: (i, 0)), pl.BlockSpec((tm, LANES), lambda i: (i, 0))],
        out_shape=[jax.ShapeDtypeStruct((m, d), BF16), jax.ShapeDtypeStruct((m, LANES), F32)],
        compiler_params=_cparams(("parallel",)),
        name="router",
    )(h, g.reshape(1, d), wp)


def _topk_kernel(a_ref, tri_ref, eye_ref, posc_ref, posr_ref, start_ref, *, cap, n_tiles):
    bits = lax.bitcast_convert_type(a_ref[...], I32)

    def search(b, lo):
        cand = lo | (jnp.int32(1) << (30 - b))
        cnt = jnp.sum((bits >= cand).astype(I32), axis=0, keepdims=True)
        return jnp.where(cnt >= cap, cand, lo)

    thr = lax.fori_loop(0, 31, search, jnp.zeros((1, LANES), I32))
    n_gt = jnp.sum((bits > thr).astype(I32), axis=0, keepdims=True)
    need_eq = (cap - n_gt).astype(F32)
    tri = tri_ref[...]
    eye = eye_ref[...]

    def tile(t, carry):
        eq_before, sel_before = carry
        r0 = pl.multiple_of(t * TOK_TILE, 16)
        b = lax.bitcast_convert_type(a_ref[pl.ds(r0, TOK_TILE), :], I32)
        gt = b > thr
        eq = b == thr
        eq_rank = eq_before + jnp.dot(tri, eq.astype(BF16), preferred_element_type=F32)
        sel = gt | (eq & (eq_rank < need_eq))
        pos = sel_before + jnp.dot(tri, sel.astype(BF16), preferred_element_type=F32)
        posm = jnp.where(sel, pos, -1.0)
        posc_ref[pl.ds(r0, TOK_TILE), :] = posm
        posr_ref[t] = lax.dot_general(eye, posm, (((1,), (1,)), ((), ())), precision=HIGHEST,
                                      preferred_element_type=F32)
        start_ref[pl.ds(t, 1), :] = sel_before
        return (eq_before + jnp.sum(eq.astype(F32), axis=0, keepdims=True),
                sel_before + jnp.sum(sel.astype(F32), axis=0, keepdims=True))

    lax.fori_loop(0, n_tiles, tile, (jnp.zeros((1, LANES), F32), jnp.zeros((1, LANES), F32)))


def expert_choice_topk(aff_pad, cap):
    rows = aff_pad.shape[0]
    n_tiles = rows // TOK_TILE
    tri = jnp.asarray(np.tril(np.ones((TOK_TILE, TOK_TILE), np.float32), -1), BF16)
    eye = jnp.asarray(np.eye(LANES, dtype=np.float32))
    return pl.pallas_call(
        functools.partial(_topk_kernel, cap=cap, n_tiles=n_tiles),
        out_shape=[jax.ShapeDtypeStruct((rows, LANES), F32), jax.ShapeDtypeStruct((n_tiles, LANES, TOK_TILE), F32),
                   jax.ShapeDtypeStruct((n_tiles, LANES), F32)],
        compiler_params=pltpu.CompilerParams(vmem_limit_bytes=VMEM_LIMIT_BYTES),
        name="expert_choice_topk",
    )(aff_pad, tri, eye)


def _gather_kernel(start_ref, u_ref, pr_ref, o_ref, acc_ref, *, seq_len, cap_pad):
    t = pl.program_id(1)

    @pl.when(t == 0)
    def _():
        acc_ref[...] = jnp.zeros(acc_ref.shape, F32)

    row = t * TOK_TILE + lax.broadcasted_iota(I32, (TOK_TILE, 1), 0)
    u = jnp.where(row < seq_len, u_ref[...], 0)
    for i in range(GATHER_EXPERTS):
        e = pl.program_id(0) * GATHER_EXPERTS + i
        ws = pl.multiple_of((start_ref[e, t] // 16) * 16, 16)
        j = (ws + lax.broadcasted_iota(I32, (WIN, 1), 0)).astype(F32)
        onehot = (j == pr_ref[i, 0]).astype(BF16)
        acc_ref[i, pl.ds(ws, WIN), :] += jnp.dot(onehot, u, preferred_element_type=F32)

    @pl.when(t == pl.num_programs(1) - 1)
    def _():
        for i in range(GATHER_EXPERTS):
            o_ref[i] = acc_ref[i, 0:cap_pad, :].astype(o_ref.dtype)


def gather_tokens(u, posm_row, tile_start, cap_pad):
    seq_len, d = u.shape
    n_tiles = posm_row.shape[1]
    grid_spec = pltpu.PrefetchScalarGridSpec(
        num_scalar_prefetch=1,
        grid=(N_EXPERTS // GATHER_EXPERTS, n_tiles),
        in_specs=[pl.BlockSpec((TOK_TILE, d), lambda e, t, s: (t, 0)),
                  pl.BlockSpec((GATHER_EXPERTS, 1, 1, TOK_TILE), lambda e, t, s: (e, t, 0, 0))],
        out_specs=pl.BlockSpec((GATHER_EXPERTS, cap_pad, d), lambda e, t, s: (e, 0, 0)),
        scratch_shapes=[pltpu.VMEM((GATHER_EXPERTS, cap_pad + WIN, d), F32)],
    )
    return pl.pallas_call(
        functools.partial(_gather_kernel, seq_len=seq_len, cap_pad=cap_pad),
        grid_spec=grid_spec,
        out_shape=jax.ShapeDtypeStruct((N_EXPERTS, cap_pad, d), BF16),
        compiler_params=_cparams(("parallel", "arbitrary")),
        name="gather_tokens",
    )(tile_start, u, posm_row)


def _combine_kernel(start_ref, h_ref, pc_ref, a_ref, g_ref, ye_ref, o_ref, buf_ref, sem_ref, *, cap_pad):
    t = pl.program_id(0)
    total = N_EXPERTS * cap_pad

    def window_start(e):
        ws = e * cap_pad + (start_ref[e, t] // 16) * 16
        return pl.multiple_of(jnp.minimum(ws, total - WIN), 16)

    def window_copy(e, slot):
        return pltpu.make_async_copy(ye_ref.at[pl.ds(window_start(e), WIN), :], buf_ref.at[slot], sem_ref.at[slot])

    for e in range(COMBINE_BUFS - 1):
        window_copy(e, e).start()
    acc = h_ref[...]
    posm = pc_ref[...]
    aff = a_ref[...]
    col = lax.broadcasted_iota(I32, (1, WIN), 1).astype(F32)
    for e in range(N_EXPERTS):
        slot = e % COMBINE_BUFS
        ahead = e + COMBINE_BUFS - 1
        if ahead < N_EXPERTS:
            window_copy(ahead, ahead % COMBINE_BUFS).start()
        window_copy(e, slot).wait()
        rel = posm[:, e:e + 1] + (e * cap_pad - window_start(e)).astype(F32)
        onehot = (rel == col).astype(BF16)
        y = jnp.dot(onehot, buf_ref[slot], preferred_element_type=F32)
        acc = acc + aff[:, e:e + 1] * y
    o_ref[...] = acc * lax.rsqrt(jnp.mean(acc * acc, axis=-1, keepdims=True) + RMS_EPS) * g_ref[...]


def moe_combine(h, posm_col, aff_pad, ye, tile_start, g):
    seq_len, d = h.shape
    n_tiles = posm_col.shape[0] // TOK_TILE
    cap_pad = ye.shape[1]
    grid_spec = pltpu.PrefetchScalarGridSpec(
        num_scalar_prefetch=1,
        grid=(n_tiles,),
        in_specs=[pl.BlockSpec((TOK_TILE, d), lambda t, s: (t, 0)),
                  pl.BlockSpec((TOK_TILE, LANES), lambda t, s: (t, 0)),
                  pl.BlockSpec((TOK_TILE, LANES), lambda t, s: (t, 0)),
                  pl.BlockSpec((1, d), lambda t, s: (0, 0)),
                  pl.BlockSpec(memory_space=pl.ANY)],
        out_specs=pl.BlockSpec((TOK_TILE, d), lambda t, s: (t, 0)),
        scratch_shapes=[pltpu.VMEM((COMBINE_BUFS, WIN, d), BF16), pltpu.SemaphoreType.DMA((COMBINE_BUFS,))],
    )
    return pl.pallas_call(
        functools.partial(_combine_kernel, cap_pad=cap_pad),
        grid_spec=grid_spec,
        out_shape=jax.ShapeDtypeStruct((seq_len, d), F32),
        compiler_params=_cparams(("arbitrary",)),
        name="moe_combine",
    )(tile_start, h, posm_col, aff_pad, g.reshape(1, d), ye.reshape(N_EXPERTS * cap_pad, d))


def kernel(x, meta_tokens, norm_mix, w_in, ml_gate_bias, ml_head_norm, hy_conv_w, hy_filt_w1, hy_filt_b1,
           hy_filt_w2, hy_filt_b2, hy_filt_freq, hy_filt_w3, hy_bias, w_branch_a, w_branch_b, w_out,
           norm_ffn, w_router, w_gate, w_up, w_down, norm_final):
    b_ = x.shape[0]
    assert b_ == 1
    h = jnp.concatenate([meta_tokens, x[0]], axis=0)
    seq_len = h.shape[0]
    tm = 912
    assert seq_len % tm == 0
    layer = 0

    w_in_l = w_in[layer]
    qkvo = norm_proj(h, norm_mix[layer], w_in_l, OFF_G, 0, tm, 1024)
    w_hm = w_in_l[:, OFF_HY:].astype(BF16)
    p_hm = norm_proj(h, norm_mix[layer], w_hm, w_hm.shape[1], 0, tm, 1024)
    w_g = jnp.pad(w_in_l[:, OFF_G:OFF_HY], ((0, 0), (0, 128 - 4 * ML_HEADS)))
    p_g = norm_proj(h, norm_mix[layer], w_g, 128, 0, tm, 128)[:, :4 * ML_HEADS]

    h_fw, h_bw = mlstm_scan(qkvo, p_g, ml_gate_bias[layer])
    y_a = mlstm_finish(h_fw, h_bw, qkvo, ml_head_norm[layer], tm, BF16)
    max_decay = math.log(HY_DECAY_TARGET) / HY_FAST_DECAY_PCT
    min_decay = math.log(HY_DECAY_TARGET) / HY_SLOW_DECAY_PCT
    deltas = jnp.abs(jnp.linspace(min_decay, max_decay, HY_WIDTH, dtype=F32))
    y_b = hyena_branch(p_hm, 0, hy_conv_w[layer], hy_filt_w1[layer], hy_filt_b1[layer], hy_filt_w2[layer],
                       hy_filt_b2[layer], hy_filt_freq[layer], hy_filt_w3[layer], hy_bias[layer], deltas,
                       seq_len, HY_BANDS, BF16)

    merged = gated_dual_mm(y_a, y_b, p_hm, 3 * HY_WIDTH, w_branch_a[layer], w_branch_b[layer], tm, 512)
    h = mm_residual(merged, w_out[layer], h, tm, 1024)

    cap = EC_CAPACITY * seq_len // N_EXPERTS
    cap_pad = -(-cap // 16) * 16
    n_tiles = -(-seq_len // TOK_TILE)
    u, aff = router(h, norm_ffn[layer], w_router[layer], tm)
    aff_pad = jnp.pad(aff, ((0, n_tiles * TOK_TILE - seq_len), (0, 0)), constant_values=-1.0)
    posm_col, posm_row, start = expert_choice_topk(aff_pad, cap)
    tile_start = start[:, :N_EXPERTS].T.astype(jnp.int32)
    posm_row = jnp.transpose(posm_row[:, :N_EXPERTS, :], (1, 0, 2))[:, :, None, :]
    xe = gather_tokens(u, posm_row, tile_start, cap_pad)
    ye = expert_ffn(xe, w_gate[layer], w_up[layer], w_down[layer], 256)
    out = moe_combine(h, posm_col, aff_pad, ye, tile_start, norm_final)
    return out[None, N_META:]
```

```python
import functools
import math

import numpy as np
import jax
import jax.numpy as jnp
from jax import lax
from jax.experimental import pallas as pl
from jax.experimental.pallas import tpu as pltpu

D_MODEL = 2048
N_META = 16
ML_HEADS = 8
ML_HEAD_DIM = D_MODEL // ML_HEADS
ML_WIDTH = ML_HEADS * ML_HEAD_DIM
ML_CHUNK = 256
HY_WIDTH = D_MODEL
HY_ORDER = 2
HY_BANDS = 16
HY_FAST_DECAY_PCT = 0.3
HY_SLOW_DECAY_PCT = 1.5
HY_DECAY_TARGET = 1e-2
N_EXPERTS = 16
EC_CAPACITY = 2
D_EXPERT = 5504
RMS_EPS = 1e-6
NEG_GATE = -1e9

OFF_G = 4 * ML_WIDTH
OFF_HY = OFF_G + 4 * ML_HEADS
OFF_MERGE = OFF_HY + 3 * HY_WIDTH

VMEM_LIMIT_BYTES = 56 * 1024 * 1024

F32 = jnp.float32
BF16 = jnp.bfloat16


def _cparams(sem):
    return pltpu.CompilerParams(dimension_semantics=sem, vmem_limit_bytes=VMEM_LIMIT_BYTES)


def _norm_proj_kernel(x_ref, g_ref, w_ref, o_ref, u_ref):
    @pl.when(pl.program_id(1) == 0)
    def _():
        x = x_ref[...]
        y = x * lax.rsqrt(jnp.mean(x * x, axis=-1, keepdims=True) + RMS_EPS)
        u_ref[...] = (y * g_ref[...]).astype(BF16)

    o_ref[...] = jnp.dot(u_ref[...], w_ref[...].astype(BF16), preferred_element_type=F32)


def norm_proj(x, g, w, n_cols, col_block0, tm, tn):
    m, d = x.shape
    grid = (m // tm, n_cols // tn)
    return pl.pallas_call(
        _norm_proj_kernel,
        grid=grid,
        in_specs=[
            pl.BlockSpec((tm, d), lambda i, j: (i, 0)),
            pl.BlockSpec((1, d), lambda i, j: (0, 0)),
            pl.BlockSpec((d, tn), lambda i, j: (0, j + col_block0)),
        ],
        out_specs=pl.BlockSpec((tm, tn), lambda i, j: (i, j)),
        out_shape=jax.ShapeDtypeStruct((m, n_cols), F32),
        scratch_shapes=[pltpu.VMEM((tm, d), BF16)],
        compiler_params=_cparams(("parallel", "arbitrary")),
        name="norm_proj",
    )(x, g.reshape(1, d), w)


def _gated_dual_mm_kernel(ya_ref, yb_ref, pa_ref, pb_ref, wa_ref, wb_ref, o_ref):
    za = jnp.dot(ya_ref[...].astype(BF16), wa_ref[...].astype(BF16), preferred_element_type=F32)
    zb = jnp.dot(yb_ref[...].astype(BF16), wb_ref[...].astype(BF16), preferred_element_type=F32)
    merged = jax.nn.sigmoid(pa_ref[...]) * za + jax.nn.sigmoid(pb_ref[...]) * zb
    o_ref[...] = merged.astype(o_ref.dtype)


def gated_dual_mm(ya, yb, p_merge, gate_col0, wa, wb, tm, tn):
    m, d = ya.shape
    nb = d // tn
    g0 = gate_col0 // tn
    return pl.pallas_call(
        _gated_dual_mm_kernel,
        grid=(m // tm, nb),
        in_specs=[pl.BlockSpec((tm, d), lambda i, j: (i, 0)),
                  pl.BlockSpec((tm, d), lambda i, j: (i, 0)),
                  pl.BlockSpec((tm, tn), lambda i, j: (i, j + g0)),
                  pl.BlockSpec((tm, tn), lambda i, j: (i, j + g0 + nb)),
                  pl.BlockSpec((d, tn), lambda i, j: (0, j)),
                  pl.BlockSpec((d, tn), lambda i, j: (0, j))],
        out_specs=pl.BlockSpec((tm, tn), lambda i, j: (i, j)),
        out_shape=jax.ShapeDtypeStruct((m, d), BF16),
        compiler_params=_cparams(("parallel", "arbitrary")),
        name="gated_dual_mm",
    )(ya, yb, p_merge, p_merge, wa, wb)


def _mm_res_kernel(a_ref, b_ref, r_ref, o_ref):
    o_ref[...] = r_ref[...] + jnp.dot(a_ref[...], b_ref[...].astype(BF16), preferred_element_type=F32)


def mm_residual(a, b, res, tm, tn):
    m, k = a.shape
    _, n = b.shape
    return pl.pallas_call(
        _mm_res_kernel,
        grid=(m // tm, n // tn),
        in_specs=[pl.BlockSpec((tm, k), lambda i, j: (i, 0)),
                  pl.BlockSpec((k, tn), lambda i, j: (0, j)),
                  pl.BlockSpec((tm, tn), lambda i, j: (i, j))],
        out_specs=pl.BlockSpec((tm, tn), lambda i, j: (i, j)),
        out_shape=jax.ShapeDtypeStruct((m, n), F32),
        compiler_params=_cparams(("parallel", "arbitrary")),
        name="mm_residual",
    )(a, b, res)


def _ffn_kernel(xe_ref, wg_ref, wu_ref, wd_ref, o_ref, acc_ref, *, tf, f_total):
    f = pl.program_id(1)

    @pl.when(f == 0)
    def _():
        acc_ref[...] = jnp.zeros(acc_ref.shape, F32)

    x = xe_ref[0]
    a = jnp.dot(x, wg_ref[0].astype(BF16), preferred_element_type=F32)
    b = jnp.dot(x, wu_ref[0].astype(BF16), preferred_element_type=F32)
    hid = (a * jax.nn.sigmoid(a)) * b
    col_ok = (f * tf + lax.broadcasted_iota(jnp.int32, (1, tf), 1)) < f_total
    row_ok = (f * tf + lax.broadcasted_iota(jnp.int32, (tf, 1), 0)) < f_total
    hid = jnp.where(col_ok, hid, 0.0).astype(BF16)
    wd = jnp.where(row_ok, wd_ref[0], 0.0).astype(BF16)
    acc_ref[...] += jnp.dot(hid, wd, preferred_element_type=F32)

    @pl.when(f == pl.num_programs(1) - 1)
    def _():
        o_ref[0] = acc_ref[...].astype(o_ref.dtype)


def expert_ffn(xe, wg, wu, wd, tf):
    e, c, d = xe.shape
    f_total = wg.shape[-1]
    nf = pl.cdiv(f_total, tf)
    return pl.pallas_call(
        functools.partial(_ffn_kernel, tf=tf, f_total=f_total),
        grid=(e, nf),
        in_specs=[pl.BlockSpec((1, c, d), lambda i, f: (i, 0, 0)),
                  pl.BlockSpec((1, d, tf), lambda i, f: (i, 0, f)),
                  pl.BlockSpec((1, d, tf), lambda i, f: (i, 0, f)),
                  pl.BlockSpec((1, tf, d), lambda i, f: (i, f, 0))],
        out_specs=pl.BlockSpec((1, c, d), lambda i, f: (i, 0, 0)),
        out_shape=jax.ShapeDtypeStruct((e, c, d), BF16),
        scratch_shapes=[pltpu.VMEM((c, d), F32)],
        compiler_params=_cparams(("parallel", "arbitrary")),
        name="expert_ffn",
    )(xe, wg, wu, wd)


FFT_N1 = 256
FFT_N2 = 72
FFT_N = FFT_N1 * FFT_N2
SEQ_SLABS = 128
SEQ_PAD = SEQ_SLABS * FFT_N2
LANES = 128
K1_HALF = 144
K1_CHUNK = 24
BUF_ROWS = K1_HALF * FFT_N2
K1_UNROLL = 8
N2_UNROLL = 3
HY_ORDERS = 2
FILT_ROWS = 1152
HIGHEST = lax.Precision.HIGHEST


def fft_tables():
    n1 = np.arange(FFT_N1)
    ang1 = 2.0 * np.pi * np.outer(n1, n1) / FFT_N1
    c1, s1 = np.cos(ang1), np.sin(ang1)
    f1_fwd = np.concatenate([c1[:K1_HALF], -s1[:K1_HALF]], axis=0)
    wgt = np.where(np.arange(K1_HALF) > FFT_N1 // 2, 0.0, 2.0)
    wgt[0] = 1.0
    wgt[FFT_N1 // 2] = 1.0
    f1_inv = np.concatenate([c1[:, :K1_HALF] * wgt, -s1[:, :K1_HALF] * wgt], axis=1) / FFT_N
    k2 = np.arange(FFT_N2)
    n2 = np.arange(FFT_N2)
    k1 = np.arange(K1_HALF)
    ang = 2.0 * np.pi * (n2[None, None, :] * (k1[:, None, None] + FFT_N1 * k2[None, :, None])) / FFT_N
    g = np.concatenate([np.cos(ang), -np.sin(ang)], axis=2)
    angt = np.transpose(ang, (0, 2, 1))
    h = np.concatenate([np.cos(angt), -np.sin(angt)], axis=2)
    return (jnp.asarray(f1_fwd, F32), jnp.asarray(f1_inv, F32), jnp.asarray(g, F32), jnp.asarray(h, F32))


def _stage1_fwd(x_ref, f1_ref, buf_ref, n_rows):
    f1 = f1_ref[:, 0:n_rows].astype(BF16)

    def body(j, carry):
        na = 2 * j
        sa = x_ref[pl.ds(na, n_rows, stride=FFT_N2), :]
        sb = x_ref[pl.ds(na + 1, n_rows, stride=FFT_N2), :]
        s = jnp.concatenate([sa, sb], axis=1).astype(BF16)
        u = jnp.dot(f1, s, preferred_element_type=F32)
        buf_ref[0, pl.ds(na, K1_HALF, stride=FFT_N2), :] = u[0:K1_HALF, 0:LANES]
        buf_ref[0, pl.ds(na + 1, K1_HALF, stride=FFT_N2), :] = u[0:K1_HALF, LANES:]
        buf_ref[1, pl.ds(na, K1_HALF, stride=FFT_N2), :] = u[K1_HALF:, 0:LANES]
        buf_ref[1, pl.ds(na + 1, K1_HALF, stride=FFT_N2), :] = u[K1_HALF:, LANES:]
        return carry

    lax.fori_loop(0, FFT_N2 // 2, body, 0, unroll=N2_UNROLL)


def _stage2_fwd(buf_ref, g_ref, row0, kk):
    ar = buf_ref[0, pl.ds(row0, FFT_N2), :]
    ai = buf_ref[1, pl.ds(row0, FFT_N2), :]
    t = jnp.concatenate([jnp.concatenate([ar, ai], axis=1),
                         jnp.concatenate([-ai, ar], axis=1)], axis=0).astype(BF16)
    x = jnp.dot(g_ref[kk].astype(BF16), t, preferred_element_type=F32)
    return x[:, 0:LANES], x[:, LANES:]


def _spectrum_kernel(k_ref, f1_ref, g_ref, o_ref, buf_ref):
    c = pl.program_id(2)

    @pl.when(c == 0)
    def _():
        _stage1_fwd(k_ref, f1_ref, buf_ref, FFT_N1)

    def body(kk, carry):
        row0 = pl.multiple_of((c * K1_CHUNK + kk) * FFT_N2, 8)
        xr, xi = _stage2_fwd(buf_ref, g_ref, row0, kk)
        o0 = pl.multiple_of(kk * FFT_N2, 8)
        o_ref[0, pl.ds(o0, FFT_N2), :] = xr
        o_ref[1, pl.ds(o0, FFT_N2), :] = xi
        return carry

    lax.fori_loop(0, K1_CHUNK, body, 0, unroll=K1_UNROLL)


def filter_spectrum(kern, f1_fwd, g):
    no, n, ch = kern.shape
    nchunk = K1_HALF // K1_CHUNK
    return pl.pallas_call(
        _spectrum_kernel,
        grid=(no, ch // LANES, nchunk),
        in_specs=[pl.BlockSpec((None, n, LANES), lambda o, b, c: (o, 0, b)),
                  pl.BlockSpec((2 * K1_HALF, FFT_N1), lambda o, b, c: (0, 0)),
                  pl.BlockSpec((K1_CHUNK, FFT_N2, 2 * FFT_N2), lambda o, b, c: (c, 0, 0))],
        out_specs=pl.BlockSpec((None, 2, K1_CHUNK * FFT_N2, LANES), lambda o, b, c: (o, 0, c, b)),
        out_shape=jax.ShapeDtypeStruct((no, 2, BUF_ROWS, ch), F32),
        scratch_shapes=[pltpu.VMEM((2, BUF_ROWS, LANES), F32)],
        compiler_params=_cparams(("parallel", "parallel", "arbitrary")),
        name="filter_spectrum",
    )(kern, f1_fwd, g)


def _conv_kernel(z_ref, kf_ref, f1_ref, f1i_ref, g_ref, h_ref, gate_ref, bias_ref, o_ref, buf_ref):
    c = pl.program_id(1)

    @pl.when(c == 0)
    def _():
        _stage1_fwd(z_ref, f1_ref, buf_ref, SEQ_SLABS)

    def body(kk, carry):
        row0 = pl.multiple_of((c * K1_CHUNK + kk) * FFT_N2, 8)
        xr, xi = _stage2_fwd(buf_ref, g_ref, row0, kk)
        k0 = pl.multiple_of(kk * FFT_N2, 8)
        kr = kf_ref[0, pl.ds(k0, FFT_N2), :]
        ki = kf_ref[1, pl.ds(k0, FFT_N2), :]
        pr = xr * kr - xi * ki
        pi = xr * ki + xi * kr
        t = jnp.concatenate([jnp.concatenate([pr, pi], axis=1),
                             jnp.concatenate([pi, -pr], axis=1)], axis=0).astype(BF16)
        q = jnp.dot(h_ref[kk].astype(BF16), t, preferred_element_type=F32)
        buf_ref[0, pl.ds(row0, FFT_N2), :] = q[:, 0:LANES]
        buf_ref[1, pl.ds(row0, FFT_N2), :] = q[:, LANES:]
        return carry

    lax.fori_loop(0, K1_CHUNK, body, 0, unroll=K1_UNROLL)

    @pl.when(c == pl.num_programs(1) - 1)
    def _():
        f1i = f1i_ref[0:SEQ_SLABS, :].astype(BF16)

        def inv_body(j, carry):
            na = 2 * j
            sra = buf_ref[0, pl.ds(na, K1_HALF, stride=FFT_N2), :]
            srb = buf_ref[0, pl.ds(na + 1, K1_HALF, stride=FFT_N2), :]
            sia = buf_ref[1, pl.ds(na, K1_HALF, stride=FFT_N2), :]
            sib = buf_ref[1, pl.ds(na + 1, K1_HALF, stride=FFT_N2), :]
            s = jnp.concatenate([jnp.concatenate([sra, srb], axis=1),
                                 jnp.concatenate([sia, sib], axis=1)], axis=0).astype(BF16)
            y = jnp.dot(f1i, s, preferred_element_type=F32)
            bias = bias_ref[...]
            for half, n2 in ((0, na), (1, na + 1)):
                rows_n2 = pl.ds(n2, SEQ_SLABS, stride=FFT_N2)
                conv = y[:, half * LANES:(half + 1) * LANES]
                o_ref[rows_n2, :] = gate_ref[rows_n2, :] * (conv + bias * z_ref[rows_n2, :])
            return carry

        lax.fori_loop(0, FFT_N2 // 2, inv_body, 0, unroll=N2_UNROLL)


def long_conv(z_src, z_block0, ch, kf, order, tables, gate_src, gate_block0, bias_row):
    f1_fwd, f1_inv, g, h = tables
    rows = z_src.shape[0]
    assert rows == SEQ_PAD
    nchunk = K1_HALF // K1_CHUNK
    return pl.pallas_call(
        _conv_kernel,
        grid=(ch // LANES, nchunk),
        in_specs=[pl.BlockSpec((rows, LANES), lambda b, c: (0, b + z_block0)),
                  pl.BlockSpec((None, 2, K1_CHUNK * FFT_N2, LANES), lambda b, c: (order, 0, c, b)),
                  pl.BlockSpec((2 * K1_HALF, FFT_N1), lambda b, c: (0, 0)),
                  pl.BlockSpec((FFT_N1, 2 * K1_HALF), lambda b, c: (0, 0)),
                  pl.BlockSpec((K1_CHUNK, FFT_N2, 2 * FFT_N2), lambda b, c: (c, 0, 0)),
                  pl.BlockSpec((K1_CHUNK, FFT_N2, 2 * FFT_N2), lambda b, c: (c, 0, 0)),
                  pl.BlockSpec((rows, LANES), lambda b, c: (0, b + gate_block0)),
                  pl.BlockSpec((1, LANES), lambda b, c: (0, b))],
        out_specs=pl.BlockSpec((rows, LANES), lambda b, c: (0, b)),
        out_shape=jax.ShapeDtypeStruct((rows, ch), F32),
        scratch_shapes=[pltpu.VMEM((2, BUF_ROWS, LANES), F32)],
        compiler_params=_cparams(("parallel", "arbitrary")),
        name="long_conv",
    )(z_src, kf, f1_fwd, f1_inv, g, h, gate_src, bias_row)


def _filter_kernel(emb_ref, w1_ref, b1_ref, w2_ref, b2_ref, fr_ref, w3_ref, dl_ref, o_ref, h_ref, *, seq_len):
    i = pl.program_id(0)

    @pl.when((pl.program_id(1) == 0) & (pl.program_id(2) == 0))
    def _():
        fr = fr_ref[...]
        h1 = jnp.sin(fr * (jnp.dot(emb_ref[...], w1_ref[...], precision=HIGHEST, preferred_element_type=F32)
                           + b1_ref[...]))
        h_ref[...] = jnp.sin(fr * (jnp.dot(h1, w2_ref[...], precision=HIGHEST, preferred_element_type=F32)
                                   + b2_ref[...]))

    h2 = h_ref[...]
    h_hi = h2.astype(BF16)
    h_lo = (h2 - h_hi.astype(F32)).astype(BF16)
    w3 = w3_ref[...]
    w_hi = w3.astype(BF16)
    w_lo = (w3 - w_hi.astype(F32)).astype(BF16)
    f = (jnp.dot(h_hi, w_hi, preferred_element_type=F32)
         + (jnp.dot(h_hi, w_lo, preferred_element_type=F32) + jnp.dot(h_lo, w_hi, preferred_element_type=F32)))
    t = emb_ref[:, 0:1]
    f = f * jnp.exp(-t * dl_ref[...])
    r = i * FILT_ROWS + lax.broadcasted_iota(jnp.int32, (FILT_ROWS, 1), 0)
    fwd = i < (FFT_N // 2) // FILT_ROWS
    lo = jnp.where(fwd, -1, FFT_N - seq_len)
    hi = jnp.where(fwd, seq_len, FFT_N)
    valid = (r > lo) & (r < hi)
    o_ref[0] = jnp.where(valid, f, 0.0)


def filter_kernels(emb, w1p, b1, w2, b2, freq, w3, deltas, seq_len, tc=512):
    ch = deltas.shape[0]
    hid = w2.shape[0]
    ncb = ch // tc
    half = (FFT_N // 2) // FILT_ROWS
    full = lambda i, o, j: (0, 0)
    return pl.pallas_call(
        functools.partial(_filter_kernel, seq_len=seq_len),
        grid=(FFT_N // FILT_ROWS, HY_ORDERS, ncb),
        in_specs=[pl.BlockSpec((FILT_ROWS, emb.shape[1]), lambda i, o, j: (i, 0)),
                  pl.BlockSpec(w1p.shape, full), pl.BlockSpec((1, hid), full),
                  pl.BlockSpec((hid, hid), full), pl.BlockSpec((1, hid), full), pl.BlockSpec((1, hid), full),
                  pl.BlockSpec((hid, tc), lambda i, o, j: (0, (2 * o + i // half) * ncb + j)),
                  pl.BlockSpec((1, tc), lambda i, o, j: (0, j))],
        out_specs=pl.BlockSpec((1, FILT_ROWS, tc), lambda i, o, j: (o, i, j)),
        out_shape=jax.ShapeDtypeStruct((HY_ORDERS, FFT_N, ch), F32),
        scratch_shapes=[pltpu.VMEM((FILT_ROWS, hid), F32)],
        compiler_params=_cparams(("parallel", "arbitrary", "arbitrary")),
        name="hyena_filters",
    )(emb, w1p, b1.reshape(1, hid), w2, b2.reshape(1, hid), freq.reshape(1, hid), w3, deltas.reshape(1, ch))


def _short_conv_kernel(u_ref, w_ref, o_ref):
    u = u_ref[...]
    rows = u.shape[0]
    r = lax.broadcasted_iota(jnp.int32, (rows, 1), 0)
    prev = jnp.where(r == 0, 0.0, pltpu.roll(u, 1, axis=0))
    nxt = jnp.where(r == rows - 1, 0.0, pltpu.roll(u, rows - 1, axis=0))
    w = w_ref[...]
    o_ref[0:rows, :] = w[0:1, :] * prev + w[1:2, :] * u + w[2:3, :] * nxt
    o_ref[rows:, :] = jnp.zeros((o_ref.shape[0] - rows, o_ref.shape[1]), F32)


def short_conv(p, conv_w, n_cols, col_block0=0):
    rows = p.shape[0]
    return pl.pallas_call(
        _short_conv_kernel,
        grid=(n_cols // LANES,),
        in_specs=[pl.BlockSpec((rows, LANES), lambda j: (0, j + col_block0)),
                  pl.BlockSpec((3, LANES), lambda j: (0, j))],
        out_specs=pl.BlockSpec((SEQ_PAD, LANES), lambda j: (0, j)),
        out_shape=jax.ShapeDtypeStruct((SEQ_PAD, n_cols), F32),
        compiler_params=_cparams(("parallel",)),
        name="short_conv",
    )(p, conv_w)


def position_features(seq_len, n_bands):
    r = np.arange(FFT_N)
    pos = jnp.asarray(np.where(r < FFT_N // 2, r, FFT_N - r).astype(np.float32)[:, None])
    t = pos / (seq_len - 1)
    bands = jnp.linspace(1e-4, n_bands - 1, n_bands, dtype=F32)[None, :]
    ang = bands * (2.0 * math.pi) * pos / seq_len
    emb = jnp.concatenate([t, jnp.cos(ang), -jnp.sin(ang)], axis=-1)
    return jnp.pad(emb, ((0, 0), (0, LANES - emb.shape[1])))


def hyena_branch(p, col_block0, conv_w, w1, b1, w2, b2, freq, w3, bias, deltas, seq_len, n_bands):
    ch = deltas.shape[0]
    assert seq_len % FFT_N2 == 0 and seq_len // FFT_N2 <= SEQ_SLABS and 2 * seq_len - 1 <= FFT_N
    tables = fft_tables()
    emb = position_features(seq_len, n_bands)
    w1p = jnp.pad(w1, ((0, LANES - w1.shape[0]), (0, 0)))
    kern = filter_kernels(emb, w1p, b1, w2, b2, freq, w3, deltas, seq_len, tc=min(512, ch))
    kf = filter_spectrum(kern, tables[0], tables[2])
    uc = short_conv(p, conv_w, 3 * ch, col_block0)
    cb = ch // LANES
    z1 = long_conv(uc, 0, ch, kf, 0, tables, uc, cb, bias[0:1])
    return long_conv(z1, 0, ch, kf, 1, tables, uc, 2 * cb, bias[1:2])


def _mlstm_direction(d, chunk, q_ref, k_ref, v_ref, g_ref, bias_ref, tri_ref, o_ref, c_ref, n_ref, m_ref, seq_len):
    t_idx = lax.broadcasted_iota(jnp.int32, (ML_CHUNK, 1), 0)
    valid = t_idx < (seq_len - chunk * ML_CHUNK)
    g = g_ref[...] + bias_ref[...]
    li_all = jnp.where(valid, g[:, 16 * d:16 * d + 8], NEG_GATE)
    lf_all = jnp.where(valid, jax.nn.log_sigmoid(g[:, 16 * d + 8:16 * d + 16]), 0.0)
    cum_all = jnp.dot(tri_ref[d], lf_all, precision=HIGHEST, preferred_element_type=F32)
    tot_all = jnp.sum(lf_all, axis=0, keepdims=True)
    a_all = tot_all - cum_all + li_all
    amax_all = jnp.max(a_all, axis=0, keepdims=True)
    tt = lax.broadcasted_iota(jnp.int32, (ML_CHUNK, ML_CHUNK), 0)
    ss = lax.broadcasted_iota(jnp.int32, (ML_CHUNK, ML_CHUNK), 1)
    visible = (ss <= tt) if d == 0 else (ss >= tt)
    ones = jnp.ones((ML_CHUNK, 1), F32)
    scale = ML_HEAD_DIM ** -0.5
    for h in range(ML_HEADS):
        sl = slice(h * ML_HEAD_DIM, (h + 1) * ML_HEAD_DIM)
        q = jnp.where(valid, q_ref[:, sl], 0.0)
        k = jnp.where(valid, k_ref[:, sl], 0.0) * scale
        v = jnp.where(valid, v_ref[:, sl], 0.0)
        qb, kb, vb = q.astype(BF16), k.astype(BF16), v.astype(BF16)
        cum = cum_all[:, h:h + 1]
        li = li_all[:, h:h + 1]
        tot = tot_all[:, h:h + 1]
        st = d * ML_HEADS + h
        m_prev = m_ref[st]
        c_prev = c_ref[st]
        n_prev = n_ref[st]
        lhs = jnp.concatenate([cum, ones], axis=1)
        rhs = jnp.concatenate([ones, li - cum], axis=1)
        log_d = lax.dot_general(lhs, rhs, (((1,), (1,)), ((), ())), precision=HIGHEST,
                                preferred_element_type=F32)
        log_d = jnp.where(visible, log_d, -jnp.inf)
        log_inter = cum + m_prev
        m_t = jnp.maximum(log_inter, jnp.max(log_d, axis=1, keepdims=True))
        s = lax.dot_general(qb, kb, (((1,), (1,)), ((), ())), preferred_element_type=F32) * jnp.exp(log_d - m_t)
        w_inter = jnp.exp(log_inter - m_t)
        num = (jnp.dot(s.astype(BF16), vb, preferred_element_type=F32)
               + w_inter * jnp.dot(qb, c_prev.astype(BF16), preferred_element_type=F32))
        den = jnp.sum(s, axis=1, keepdims=True) + w_inter * jnp.sum(q * n_prev, axis=1, keepdims=True)
        o_ref[:, sl] = num / jnp.maximum(jnp.abs(den), jnp.exp(-m_t))
        m_new = jnp.maximum(tot + m_prev, amax_all[:, h:h + 1])
        decay = jnp.exp(tot + m_prev - m_new)
        kw = k * jnp.exp(a_all[:, h:h + 1] - m_new)
        c_ref[st] = decay * c_prev + lax.dot_general(kw.astype(BF16), vb, (((0,), (0,)), ((), ())),
                                                     preferred_element_type=F32)
        n_ref[st] = decay * n_prev + jnp.sum(kw, axis=0, keepdims=True)
        m_ref[st] = m_new


def _mlstm_kernel(qf, kf, vf, gf, qb, kb, vb, gb, bias_ref, tri_ref, of_ref, ob_ref, c_ref, n_ref, m_ref,
                  *, n_chunks, seq_len):
    i = pl.program_id(0)

    @pl.when(i == 0)
    def _():
        c_ref[...] = jnp.zeros(c_ref.shape, F32)
        n_ref[...] = jnp.zeros(n_ref.shape, F32)
        m_ref[...] = jnp.zeros(m_ref.shape, F32)

    _mlstm_direction(0, i, qf, kf, vf, gf, bias_ref, tri_ref, of_ref, c_ref, n_ref, m_ref, seq_len)
    _mlstm_direction(1, n_chunks - 1 - i, qb, kb, vb, gb, bias_ref, tri_ref, ob_ref, c_ref, n_ref, m_ref, seq_len)


def mlstm_scan(qkvo, p_gate, gate_bias):
    seq_len = qkvo.shape[0]
    w = ML_WIDTH
    n_chunks = pl.cdiv(seq_len, ML_CHUNK)
    tri = np.tril(np.ones((ML_CHUNK, ML_CHUNK), np.float32))
    tri = jnp.asarray(np.stack([tri, tri.T]))
    fw = lambda col: (lambda i: (i, col))
    bw = lambda col: (lambda i: (n_chunks - 1 - i, col))
    blk = lambda im: pl.BlockSpec((ML_CHUNK, w), im)
    gblk = lambda im: pl.BlockSpec((ML_CHUNK, 4 * ML_HEADS), im)
    return pl.pallas_call(
        functools.partial(_mlstm_kernel, n_chunks=n_chunks, seq_len=seq_len),
        grid=(n_chunks,),
        in_specs=[blk(fw(0)), blk(fw(1)), blk(fw(2)), gblk(fw(0)),
                  blk(bw(0)), blk(bw(1)), blk(bw(2)), gblk(bw(0)),
                  pl.BlockSpec((1, 4 * ML_HEADS), lambda i: (0, 0)),
                  pl.BlockSpec((2, ML_CHUNK, ML_CHUNK), lambda i: (0, 0, 0))],
        out_specs=[blk(fw(0)), blk(bw(0))],
        out_shape=[jax.ShapeDtypeStruct((seq_len, w), F32), jax.ShapeDtypeStruct((seq_len, w), F32)],
        scratch_shapes=[pltpu.VMEM((2 * ML_HEADS, ML_HEAD_DIM, ML_HEAD_DIM), F32),
                        pltpu.VMEM((2 * ML_HEADS, 1, ML_HEAD_DIM), F32),
                        pltpu.VMEM((2 * ML_HEADS, 1, 1), F32)],
        compiler_params=_cparams(("arbitrary",)),
        name="mlstm_scan",
    )(qkvo, qkvo, qkvo, p_gate, qkvo, qkvo, qkvo, p_gate, gate_bias.reshape(1, -1), tri)


def _mlstm_finish_kernel(hf_ref, hb_ref, o_ref, g_ref, y_ref):
    for h in range(ML_HEADS):
        sl = slice(h * ML_HEAD_DIM, (h + 1) * ML_HEAD_DIM)
        x = hf_ref[:, sl] + hb_ref[:, sl]
        hn = x * lax.rsqrt(jnp.mean(x * x, axis=-1, keepdims=True) + RMS_EPS) * g_ref[:, sl]
        y_ref[:, sl] = (jax.nn.sigmoid(o_ref[:, sl]) * hn).astype(y_ref.dtype)


def mlstm_finish(h_fw, h_bw, qkvo, head_norm, tm, out_dtype):
    seq_len, w = h_fw.shape
    row = lambda i: (i, 0)
    return pl.pallas_call(
        _mlstm_finish_kernel,
        grid=(seq_len // tm,),
        in_specs=[pl.BlockSpec((tm, w), row), pl.BlockSpec((tm, w), row),
                  pl.BlockSpec((tm, w), lambda i: (i, 3)), pl.BlockSpec((1, w), lambda i: (0, 0))],
        out_specs=pl.BlockSpec((tm, w), row),
        out_shape=jax.ShapeDtypeStruct((seq_len, w), out_dtype),
        compiler_params=_cparams(("parallel",)),
        name="mlstm_finish",
    )(h_fw, h_bw, qkvo, head_norm.reshape(1, w))


TOK_TILE = 240
WIN = 256
I32 = jnp.int32
COMBINE_BUFS = 4
GATHER_EXPERTS = 2


def _router_kernel(h_ref, g_ref, w_ref, u_ref, a_ref):
    x = h_ref[...]
    u = (x * lax.rsqrt(jnp.mean(x * x, axis=-1, keepdims=True) + RMS_EPS) * g_ref[...]).astype(BF16)
    u_ref[...] = u
    logits = jnp.dot(u, w_ref[...].astype(BF16), preferred_element_type=F32)
    lane = lax.broadcasted_iota(I32, logits.shape, 1)
    logits = jnp.where(lane < N_EXPERTS, logits, -jnp.inf)
    z = jnp.exp(logits - jnp.max(logits, axis=-1, keepdims=True))
    a_ref[...] = z / jnp.sum(z, axis=-1, keepdims=True)


def router(h, g, w_router, tm):
    m, d = h.shape
    wp = jnp.pad(w_router, ((0, 0), (0, LANES - w_router.shape[1])))
    return pl.pallas_call(
        _router_kernel,
        grid=(m // tm,),
        in_specs=[pl.BlockSpec((tm, d), lambda i: (i, 0)), pl.BlockSpec((1, d), lambda i: (0, 0)),
                  pl.BlockSpec((d, LANES), lambda i: (0, 0))],
        out_specs=[pl.BlockSpec((tm, d), lambda i: (i, 0)), pl.BlockSpec((tm, LANES), lambda i: (i, 0))],
        out_shape=[jax.ShapeDtypeStruct((m, d), BF16), jax.ShapeDtypeStruct((m, LANES), F32)],
        compiler_params=_cparams(("parallel",)),
        name="router",
    )(h, g.reshape(1, d), wp)


def _topk_kernel(a_ref, tri_ref, eye_ref, posc_ref, posr_ref, start_ref, *, cap, n_tiles):
    bits = lax.bitcast_convert_type(a_ref[...], I32)

    def search(b, lo):
        cand = lo | (jnp.int32(1) << (30 - b))
        cnt = jnp.sum((bits >= cand).astype(I32), axis=0, keepdims=True)
        return jnp.where(cnt >= cap, cand, lo)

    thr = lax.fori_loop(0, 31, search, jnp.zeros((1, LANES), I32))
    n_gt = jnp.sum((bits > thr).astype(I32), axis=0, keepdims=True)
    need_eq = (cap - n_gt).astype(F32)
    tri = tri_ref[...]
    eye = eye_ref[...]

    def tile(t, carry):
        eq_before, sel_before = carry
        r0 = pl.multiple_of(t * TOK_TILE, 16)
        b = lax.bitcast_convert_type(a_ref[pl.ds(r0, TOK_TILE), :], I32)
        gt = b > thr
        eq = b == thr
        eq_rank = eq_before + jnp.dot(tri, eq.astype(BF16), preferred_element_type=F32)
        sel = gt | (eq & (eq_rank < need_eq))
        pos = sel_before + jnp.dot(tri, sel.astype(BF16), preferred_element_type=F32)
        posm = jnp.where(sel, pos, -1.0)
        posc_ref[pl.ds(r0, TOK_TILE), :] = posm
        posr_ref[t] = lax.dot_general(eye, posm, (((1,), (1,)), ((), ())), precision=HIGHEST,
                                      preferred_element_type=F32)
        start_ref[pl.ds(t, 1), :] = sel_before
        return (eq_before + jnp.sum(eq.astype(F32), axis=0, keepdims=True),
                sel_before + jnp.sum(sel.astype(F32), axis=0, keepdims=True))

    lax.fori_loop(0, n_tiles, tile, (jnp.zeros((1, LANES), F32), jnp.zeros((1, LANES), F32)))


def expert_choice_topk(aff_pad, cap):
    rows = aff_pad.shape[0]
    n_tiles = rows // TOK_TILE
    tri = jnp.asarray(np.tril(np.ones((TOK_TILE, TOK_TILE), np.float32), -1), BF16)
    eye = jnp.asarray(np.eye(LANES, dtype=np.float32))
    return pl.pallas_call(
        functools.partial(_topk_kernel, cap=cap, n_tiles=n_tiles),
        out_shape=[jax.ShapeDtypeStruct((rows, LANES), F32), jax.ShapeDtypeStruct((n_tiles, LANES, TOK_TILE), F32),
                   jax.ShapeDtypeStruct((n_tiles, LANES), F32)],
        compiler_params=pltpu.CompilerParams(vmem_limit_bytes=VMEM_LIMIT_BYTES),
        name="expert_choice_topk",
    )(aff_pad, tri, eye)


def _gather_kernel(start_ref, u_ref, pr_ref, o_ref, acc_ref, *, seq_len, cap_pad):
    t = pl.program_id(1)

    @pl.when(t == 0)
    def _():
        acc_ref[...] = jnp.zeros(acc_ref.shape, F32)

    row = t * TOK_TILE + lax.broadcasted_iota(I32, (TOK_TILE, 1), 0)
    u = jnp.where(row < seq_len, u_ref[...], 0)
    for i in range(GATHER_EXPERTS):
        e = pl.program_id(0) * GATHER_EXPERTS + i
        ws = pl.multiple_of((start_ref[e, t] // 16) * 16, 16)
        j = (ws + lax.broadcasted_iota(I32, (WIN, 1), 0)).astype(F32)
        onehot = (j == pr_ref[i, 0]).astype(BF16)
        acc_ref[i, pl.ds(ws, WIN), :] += jnp.dot(onehot, u, preferred_element_type=F32)

    @pl.when(t == pl.num_programs(1) - 1)
    def _():
        for i in range(GATHER_EXPERTS):
            o_ref[i] = acc_ref[i, 0:cap_pad, :].astype(o_ref.dtype)


def gather_tokens(u, posm_row, tile_start, cap_pad):
    seq_len, d = u.shape
    n_tiles = posm_row.shape[1]
    grid_spec = pltpu.PrefetchScalarGridSpec(
        num_scalar_prefetch=1,
        grid=(N_EXPERTS // GATHER_EXPERTS, n_tiles),
        in_specs=[pl.BlockSpec((TOK_TILE, d), lambda e, t, s: (t, 0)),
                  pl.BlockSpec((GATHER_EXPERTS, 1, 1, TOK_TILE), lambda e, t, s: (e, t, 0, 0))],
        out_specs=pl.BlockSpec((GATHER_EXPERTS, cap_pad, d), lambda e, t, s: (e, 0, 0)),
        scratch_shapes=[pltpu.VMEM((GATHER_EXPERTS, cap_pad + WIN, d), F32)],
    )
    return pl.pallas_call(
        functools.partial(_gather_kernel, seq_len=seq_len, cap_pad=cap_pad),
        grid_spec=grid_spec,
        out_shape=jax.ShapeDtypeStruct((N_EXPERTS, cap_pad, d), BF16),
        compiler_params=_cparams(("parallel", "arbitrary")),
        name="gather_tokens",
    )(tile_start, u, posm_row)


def _combine_kernel(start_ref, h_ref, pc_ref, a_ref, g_ref, ye_ref, o_ref, buf_ref, sem_ref, *, cap_pad):
    t = pl.program_id(0)
    total = N_EXPERTS * cap_pad

    def window_start(e):
        ws = e * cap_pad + (start_ref[e, t] // 16) * 16
        return pl.multiple_of(jnp.minimum(ws, total - WIN), 16)

    def window_copy(e, slot):
        return pltpu.make_async_copy(ye_ref.at[pl.ds(window_start(e), WIN), :], buf_ref.at[slot], sem_ref.at[slot])

    for e in range(COMBINE_BUFS - 1):
        window_copy(e, e).start()
    acc = h_ref[...]
    posm = pc_ref[...]
    aff = a_ref[...]
    col = lax.broadcasted_iota(I32, (1, WIN), 1).astype(F32)
    for e in range(N_EXPERTS):
        slot = e % COMBINE_BUFS
        ahead = e + COMBINE_BUFS - 1
        if ahead < N_EXPERTS:
            window_copy(ahead, ahead % COMBINE_BUFS).start()
        window_copy(e, slot).wait()
        rel = posm[:, e:e + 1] + (e * cap_pad - window_start(e)).astype(F32)
        onehot = (rel == col).astype(BF16)
        y = jnp.dot(onehot, buf_ref[slot], preferred_element_type=F32)
        acc = acc + aff[:, e:e + 1] * y
    o_ref[...] = acc * lax.rsqrt(jnp.mean(acc * acc, axis=-1, keepdims=True) + RMS_EPS) * g_ref[...]


def moe_combine(h, posm_col, aff_pad, ye, tile_start, g):
    seq_len, d = h.shape
    n_tiles = posm_col.shape[0] // TOK_TILE
    cap_pad = ye.shape[1]
    grid_spec = pltpu.PrefetchScalarGridSpec(
        num_scalar_prefetch=1,
        grid=(n_tiles,),
        in_specs=[pl.BlockSpec((TOK_TILE, d), lambda t, s: (t, 0)),
                  pl.BlockSpec((TOK_TILE, LANES), lambda t, s: (t, 0)),
                  pl.BlockSpec((TOK_TILE, LANES), lambda t, s: (t, 0)),
                  pl.BlockSpec((1, d), lambda t, s: (0, 0)),
                  pl.BlockSpec(memory_space=pl.ANY)],
        out_specs=pl.BlockSpec((TOK_TILE, d), lambda t, s: (t, 0)),
        scratch_shapes=[pltpu.VMEM((COMBINE_BUFS, WIN, d), BF16), pltpu.SemaphoreType.DMA((COMBINE_BUFS,))],
    )
    return pl.pallas_call(
        functools.partial(_combine_kernel, cap_pad=cap_pad),
        grid_spec=grid_spec,
        out_shape=jax.ShapeDtypeStruct((seq_len, d), F32),
        compiler_params=_cparams(("arbitrary",)),
        name="moe_combine",
    )(tile_start, h, posm_col, aff_pad, g.reshape(1, d), ye.reshape(N_EXPERTS * cap_pad, d))


def kernel(x, meta_tokens, norm_mix, w_in, ml_gate_bias, ml_head_norm, hy_conv_w, hy_filt_w1, hy_filt_b1,
           hy_filt_w2, hy_filt_b2, hy_filt_freq, hy_filt_w3, hy_bias, w_branch_a, w_branch_b, w_out,
           norm_ffn, w_router, w_gate, w_up, w_down, norm_final):
    b_ = x.shape[0]
    assert b_ == 1
    h = jnp.concatenate([meta_tokens, x[0]], axis=0)
    seq_len = h.shape[0]
    tm = 912
    assert seq_len % tm == 0
    layer = 0

    w_in_l = w_in[layer]
    qkvo = norm_proj(h, norm_mix[layer], w_in_l[:, :OFF_G].astype(BF16), OFF_G, 0, tm, 1024)
    w_hm = w_in_l[:, OFF_HY:].astype(BF16)
    p_hm = norm_proj(h, norm_mix[layer], w_hm, w_hm.shape[1], 0, tm, 1024)
    w_g = jnp.pad(w_in_l[:, OFF_G:OFF_HY], ((0, 0), (0, 128 - 4 * ML_HEADS)))
    p_g = norm_proj(h, norm_mix[layer], w_g, 128, 0, tm, 128)[:, :4 * ML_HEADS]

    h_fw, h_bw = mlstm_scan(qkvo, p_g, ml_gate_bias[layer])
    y_a = mlstm_finish(h_fw, h_bw, qkvo, ml_head_norm[layer], tm, BF16)
    max_decay = math.log(HY_DECAY_TARGET) / HY_FAST_DECAY_PCT
    min_decay = math.log(HY_DECAY_TARGET) / HY_SLOW_DECAY_PCT
    deltas = jnp.abs(jnp.linspace(min_decay, max_decay, HY_WIDTH, dtype=F32))
    y_b = hyena_branch(p_hm, 0, hy_conv_w[layer], hy_filt_w1[layer], hy_filt_b1[layer], hy_filt_w2[layer],
                       hy_filt_b2[layer], hy_filt_freq[layer], hy_filt_w3[layer], hy_bias[layer], deltas,
                       seq_len, HY_BANDS)

    merged = gated_dual_mm(y_a, y_b, p_hm, 3 * HY_WIDTH, w_branch_a[layer], w_branch_b[layer], tm, 512)
    h = mm_residual(merged, w_out[layer], h, tm, 1024)

    cap = EC_CAPACITY * seq_len // N_EXPERTS
    cap_pad = -(-cap // 16) * 16
    n_tiles = -(-seq_len // TOK_TILE)
    u, aff = router(h, norm_ffn[layer], w_router[layer], tm)
    aff_pad = jnp.pad(aff, ((0, n_tiles * TOK_TILE - seq_len), (0, 0)), constant_values=-1.0)
    posm_col, posm_row, start = expert_choice_topk(aff_pad, cap)
    tile_start = start[:, :N_EXPERTS].T.astype(jnp.int32)
    posm_row = jnp.transpose(posm_row[:, :N_EXPERTS, :], (1, 0, 2))[:, :, None, :]
    xe = gather_tokens(u, posm_row, tile_start, cap_pad)
    ye = expert_ffn(xe, w_gate[layer], w_up[layer], w_down[layer], 256)
    out = moe_combine(h, posm_col, aff_pad, ye, tile_start, norm_final)
    return out[None, N_META:]
```

```python
import functools
import math

import numpy as np
import jax
import jax.numpy as jnp
from jax import lax
from jax.experimental import pallas as pl
from jax.experimental.pallas import tpu as pltpu

D_MODEL = 2048
N_META = 16
ML_HEADS = 8
ML_HEAD_DIM = D_MODEL // ML_HEADS
ML_WIDTH = ML_HEADS * ML_HEAD_DIM
ML_CHUNK = 256
HY_WIDTH = D_MODEL
HY_ORDER = 2
HY_BANDS = 16
HY_FAST_DECAY_PCT = 0.3
HY_SLOW_DECAY_PCT = 1.5
HY_DECAY_TARGET = 1e-2
N_EXPERTS = 16
EC_CAPACITY = 2
D_EXPERT = 5504
RMS_EPS = 1e-6
NEG_GATE = -1e9

OFF_G = 4 * ML_WIDTH
OFF_HY = OFF_G + 4 * ML_HEADS
OFF_MERGE = OFF_HY + 3 * HY_WIDTH

VMEM_LIMIT_BYTES = 56 * 1024 * 1024

F32 = jnp.float32
BF16 = jnp.bfloat16


def _cparams(sem):
    return pltpu.CompilerParams(dimension_semantics=sem, vmem_limit_bytes=VMEM_LIMIT_BYTES)


def _norm_proj_kernel(x_ref, g_ref, w_ref, o_ref, u_ref):
    @pl.when(pl.program_id(1) == 0)
    def _():
        x = x_ref[...]
        y = x * lax.rsqrt(jnp.mean(x * x, axis=-1, keepdims=True) + RMS_EPS)
        u_ref[...] = (y * g_ref[...]).astype(BF16)

    o_ref[...] = lax.dot_general(u_ref[...], w_ref[...].astype(BF16), (((1,), (1,)), ((), ())),
                                 preferred_element_type=F32)


def norm_proj(x, g, wt, n_cols, row_block0, tm, tn):
    m, d = x.shape
    grid = (m // tm, n_cols // tn)
    return pl.pallas_call(
        _norm_proj_kernel,
        grid=grid,
        in_specs=[
            pl.BlockSpec((tm, d), lambda i, j: (i, 0)),
            pl.BlockSpec((1, d), lambda i, j: (0, 0)),
            pl.BlockSpec((tn, d), lambda i, j: (j + row_block0, 0)),
        ],
        out_specs=pl.BlockSpec((tm, tn), lambda i, j: (i, j)),
        out_shape=jax.ShapeDtypeStruct((m, n_cols), F32),
        scratch_shapes=[pltpu.VMEM((tm, d), BF16)],
        compiler_params=_cparams(("parallel", "arbitrary")),
        name="norm_proj",
    )(x, g.reshape(1, d), wt)


def _gated_dual_mm_kernel(ya_ref, yb_ref, pa_ref, pb_ref, wa_ref, wb_ref, o_ref):
    za = jnp.dot(ya_ref[...].astype(BF16), wa_ref[...].astype(BF16), preferred_element_type=F32)
    zb = jnp.dot(yb_ref[...].astype(BF16), wb_ref[...].astype(BF16), preferred_element_type=F32)
    merged = jax.nn.sigmoid(pa_ref[...]) * za + jax.nn.sigmoid(pb_ref[...]) * zb
    o_ref[...] = merged.astype(o_ref.dtype)


def gated_dual_mm(ya, yb, p_merge, gate_col0, wa, wb, tm, tn):
    m, d = ya.shape
    nb = d // tn
    g0 = gate_col0 // tn
    return pl.pallas_call(
        _gated_dual_mm_kernel,
        grid=(m // tm, nb),
        in_specs=[pl.BlockSpec((tm, d), lambda i, j: (i, 0)),
                  pl.BlockSpec((tm, d), lambda i, j: (i, 0)),
                  pl.BlockSpec((tm, tn), lambda i, j: (i, j + g0)),
                  pl.BlockSpec((tm, tn), lambda i, j: (i, j + g0 + nb)),
                  pl.BlockSpec((d, tn), lambda i, j: (0, j)),
                  pl.BlockSpec((d, tn), lambda i, j: (0, j))],
        out_specs=pl.BlockSpec((tm, tn), lambda i, j: (i, j)),
        out_shape=jax.ShapeDtypeStruct((m, d), BF16),
        compiler_params=_cparams(("parallel", "arbitrary")),
        name="gated_dual_mm",
    )(ya, yb, p_merge, p_merge, wa, wb)


def _mm_res_kernel(a_ref, b_ref, r_ref, o_ref):
    o_ref[...] = r_ref[...] + jnp.dot(a_ref[...], b_ref[...].astype(BF16), preferred_element_type=F32)


def mm_residual(a, b, res, tm, tn):
    m, k = a.shape
    _, n = b.shape
    return pl.pallas_call(
        _mm_res_kernel,
        grid=(m // tm, n // tn),
        in_specs=[pl.BlockSpec((tm, k), lambda i, j: (i, 0)),
                  pl.BlockSpec((k, tn), lambda i, j: (0, j)),
                  pl.BlockSpec((tm, tn), lambda i, j: (i, j))],
        out_specs=pl.BlockSpec((tm, tn), lambda i, j: (i, j)),
        out_shape=jax.ShapeDtypeStruct((m, n), F32),
        compiler_params=_cparams(("parallel", "arbitrary")),
        name="mm_residual",
    )(a, b, res)


def _ffn_kernel(xe_ref, wg_ref, wu_ref, wd_ref, o_ref, acc_ref, *, tf, f_total):
    f = pl.program_id(1)

    @pl.when(f == 0)
    def _():
        acc_ref[...] = jnp.zeros(acc_ref.shape, F32)

    x = xe_ref[0]
    a = jnp.dot(x, wg_ref[0].astype(BF16), preferred_element_type=F32)
    b = jnp.dot(x, wu_ref[0].astype(BF16), preferred_element_type=F32)
    hid = (a * jax.nn.sigmoid(a)) * b
    col_ok = (f * tf + lax.broadcasted_iota(jnp.int32, (1, tf), 1)) < f_total
    row_ok = (f * tf + lax.broadcasted_iota(jnp.int32, (tf, 1), 0)) < f_total
    hid = jnp.where(col_ok, hid, 0.0).astype(BF16)
    wd = jnp.where(row_ok, wd_ref[0], 0.0).astype(BF16)
    acc_ref[...] += jnp.dot(hid, wd, preferred_element_type=F32)

    @pl.when(f == pl.num_programs(1) - 1)
    def _():
        o_ref[0] = acc_ref[...].astype(o_ref.dtype)


def expert_ffn(xe, wg, wu, wd, tf):
    e, c, d = xe.shape
    f_total = wg.shape[-1]
    nf = pl.cdiv(f_total, tf)
    return pl.pallas_call(
        functools.partial(_ffn_kernel, tf=tf, f_total=f_total),
        grid=(e, nf),
        in_specs=[pl.BlockSpec((1, c, d), lambda i, f: (i, 0, 0)),
                  pl.BlockSpec((1, d, tf), lambda i, f: (i, 0, f)),
                  pl.BlockSpec((1, d, tf), lambda i, f: (i, 0, f)),
                  pl.BlockSpec((1, tf, d), lambda i, f: (i, f, 0))],
        out_specs=pl.BlockSpec((1, c, d), lambda i, f: (i, 0, 0)),
        out_shape=jax.ShapeDtypeStruct((e, c, d), BF16),
        scratch_shapes=[pltpu.VMEM((c, d), F32)],
        compiler_params=_cparams(("parallel", "arbitrary")),
        name="expert_ffn",
    )(xe, wg, wu, wd)


FFT_N1 = 256
FFT_N2 = 72
FFT_N = FFT_N1 * FFT_N2
SEQ_SLABS = 128
SEQ_PAD = SEQ_SLABS * FFT_N2
LANES = 128
K1_HALF = 144
K1_CHUNK = 24
SPEC_CHUNK = 48
BUF_ROWS = K1_HALF * FFT_N2
K1_UNROLL = 8
N2_UNROLL = 3
HY_ORDERS = 2
FILT_ROWS = 1152
HIGHEST = lax.Precision.HIGHEST


def fft_tables():
    n1 = np.arange(FFT_N1)
    ang1 = 2.0 * np.pi * np.outer(n1, n1) / FFT_N1
    c1, s1 = np.cos(ang1), np.sin(ang1)
    f1_fwd = np.concatenate([c1[:K1_HALF], -s1[:K1_HALF]], axis=0)
    wgt = np.where(np.arange(K1_HALF) > FFT_N1 // 2, 0.0, 2.0)
    wgt[0] = 1.0
    wgt[FFT_N1 // 2] = 1.0
    f1_inv = np.concatenate([c1[:, :K1_HALF] * wgt, -s1[:, :K1_HALF] * wgt], axis=1) / FFT_N
    k2 = np.arange(FFT_N2)
    n2 = np.arange(FFT_N2)
    k1 = np.arange(K1_HALF)
    ang = 2.0 * np.pi * (n2[None, None, :] * (k1[:, None, None] + FFT_N1 * k2[None, :, None])) / FFT_N
    g = np.concatenate([np.cos(ang), -np.sin(ang)], axis=2)
    angt = np.transpose(ang, (0, 2, 1))
    h = np.concatenate([np.cos(angt), -np.sin(angt)], axis=2)
    return (jnp.asarray(f1_fwd, F32), jnp.asarray(f1_inv, F32), jnp.asarray(g, F32), jnp.asarray(h, F32))


def _stage1_fwd(x_ref, f1_ref, buf_ref, n_rows):
    f1 = f1_ref[:, 0:n_rows].astype(BF16)

    def body(j, carry):
        na = 2 * j
        sa = x_ref[pl.ds(na, n_rows, stride=FFT_N2), :]
        sb = x_ref[pl.ds(na + 1, n_rows, stride=FFT_N2), :]
        s = jnp.concatenate([sa, sb], axis=1).astype(BF16)
        u = jnp.dot(f1, s, preferred_element_type=F32)
        buf_ref[0, pl.ds(na, K1_HALF, stride=FFT_N2), :] = u[0:K1_HALF, 0:LANES]
        buf_ref[0, pl.ds(na + 1, K1_HALF, stride=FFT_N2), :] = u[0:K1_HALF, LANES:]
        buf_ref[1, pl.ds(na, K1_HALF, stride=FFT_N2), :] = u[K1_HALF:, 0:LANES]
        buf_ref[1, pl.ds(na + 1, K1_HALF, stride=FFT_N2), :] = u[K1_HALF:, LANES:]
        return carry

    lax.fori_loop(0, FFT_N2 // 2, body, 0, unroll=N2_UNROLL)


def _stage2_fwd(buf_ref, g_ref, row0, kk):
    ar = buf_ref[0, pl.ds(row0, FFT_N2), :]
    ai = buf_ref[1, pl.ds(row0, FFT_N2), :]
    t = jnp.concatenate([jnp.concatenate([ar, ai], axis=1),
                         jnp.concatenate([-ai, ar], axis=1)], axis=0).astype(BF16)
    x = jnp.dot(g_ref[kk].astype(BF16), t, preferred_element_type=F32)
    return x[:, 0:LANES], x[:, LANES:]


def _spectrum_kernel(k_ref, f1_ref, g_ref, o_ref, buf_ref):
    c = pl.program_id(2)

    @pl.when(c == 0)
    def _():
        _stage1_fwd(k_ref, f1_ref, buf_ref, FFT_N1)

    def body(kk, carry):
        row0 = pl.multiple_of((c * SPEC_CHUNK + kk) * FFT_N2, 8)
        xr, xi = _stage2_fwd(buf_ref, g_ref, row0, kk)
        o0 = pl.multiple_of(kk * FFT_N2, 8)
        o_ref[0, pl.ds(o0, FFT_N2), :] = xr
        o_ref[1, pl.ds(o0, FFT_N2), :] = xi
        return carry

    lax.fori_loop(0, SPEC_CHUNK, body, 0, unroll=K1_UNROLL)


def filter_spectrum(kern, f1_fwd, g):
    no, n, ch = kern.shape
    nchunk = K1_HALF // SPEC_CHUNK
    return pl.pallas_call(
        _spectrum_kernel,
        grid=(no, ch // LANES, nchunk),
        in_specs=[pl.BlockSpec((None, n, LANES), lambda o, b, c: (o, 0, b)),
                  pl.BlockSpec((2 * K1_HALF, FFT_N1), lambda o, b, c: (0, 0)),
                  pl.BlockSpec((SPEC_CHUNK, FFT_N2, 2 * FFT_N2), lambda o, b, c: (c, 0, 0))],
        out_specs=pl.BlockSpec((None, 2, SPEC_CHUNK * FFT_N2, LANES), lambda o, b, c: (o, 0, c, b)),
        out_shape=jax.ShapeDtypeStruct((no, 2, BUF_ROWS, ch), F32),
        scratch_shapes=[pltpu.VMEM((2, BUF_ROWS, LANES), F32)],
        compiler_params=_cparams(("parallel", "parallel", "arbitrary")),
        name="filter_spectrum",
    )(kern, f1_fwd, g)


def _conv_kernel(z_ref, kf_ref, f1_ref, f1i_ref, g_ref, h_ref, gate_ref, bias_ref, o_ref, buf_ref):
    c = pl.program_id(1)

    @pl.when(c == 0)
    def _():
        _stage1_fwd(z_ref, f1_ref, buf_ref, SEQ_SLABS)

    def body(kk, carry):
        row0 = pl.multiple_of((c * K1_CHUNK + kk) * FFT_N2, 8)
        xr, xi = _stage2_fwd(buf_ref, g_ref, row0, kk)
        k0 = pl.multiple_of(kk * FFT_N2, 8)
        kr = kf_ref[0, pl.ds(k0, FFT_N2), :]
        ki = kf_ref[1, pl.ds(k0, FFT_N2), :]
        pr = xr * kr - xi * ki
        pi = xr * ki + xi * kr
        t = jnp.concatenate([jnp.concatenate([pr, pi], axis=1),
                             jnp.concatenate([pi, -pr], axis=1)], axis=0).astype(BF16)
        q = jnp.dot(h_ref[kk].astype(BF16), t, preferred_element_type=F32)
        buf_ref[0, pl.ds(row0, FFT_N2), :] = q[:, 0:LANES]
        buf_ref[1, pl.ds(row0, FFT_N2), :] = q[:, LANES:]
        return carry

    lax.fori_loop(0, K1_CHUNK, body, 0, unroll=K1_UNROLL)

    @pl.when(c == pl.num_programs(1) - 1)
    def _():
        f1i = f1i_ref[0:SEQ_SLABS, :].astype(BF16)

        def inv_body(j, carry):
            na = 2 * j
            sra = buf_ref[0, pl.ds(na, K1_HALF, stride=FFT_N2), :]
            srb = buf_ref[0, pl.ds(na + 1, K1_HALF, stride=FFT_N2), :]
            sia = buf_ref[1, pl.ds(na, K1_HALF, stride=FFT_N2), :]
            sib = buf_ref[1, pl.ds(na + 1, K1_HALF, stride=FFT_N2), :]
            s = jnp.concatenate([jnp.concatenate([sra, srb], axis=1),
                                 jnp.concatenate([sia, sib], axis=1)], axis=0).astype(BF16)
            y = jnp.dot(f1i, s, preferred_element_type=F32)
            bias = bias_ref[...]
            for half, n2 in ((0, na), (1, na + 1)):
                rows_n2 = pl.ds(n2, SEQ_SLABS, stride=FFT_N2)
                conv = y[:, half * LANES:(half + 1) * LANES]
                o_ref[rows_n2, :] = gate_ref[rows_n2, :] * (conv + bias * z_ref[rows_n2, :])
            return carry

        lax.fori_loop(0, FFT_N2 // 2, inv_body, 0, unroll=N2_UNROLL)


def long_conv(z_src, z_block0, ch, kf, order, tables, gate_src, gate_block0, bias_row):
    f1_fwd, f1_inv, g, h = tables
    rows = z_src.shape[0]
    assert rows == SEQ_PAD
    nchunk = K1_HALF // K1_CHUNK
    return pl.pallas_call(
        _conv_kernel,
        grid=(ch // LANES, nchunk),
        in_specs=[pl.BlockSpec((rows, LANES), lambda b, c: (0, b + z_block0)),
                  pl.BlockSpec((None, 2, K1_CHUNK * FFT_N2, LANES), lambda b, c: (order, 0, c, b)),
                  pl.BlockSpec((2 * K1_HALF, FFT_N1), lambda b, c: (0, 0)),
                  pl.BlockSpec((FFT_N1, 2 * K1_HALF), lambda b, c: (0, 0)),
                  pl.BlockSpec((K1_CHUNK, FFT_N2, 2 * FFT_N2), lambda b, c: (c, 0, 0)),
                  pl.BlockSpec((K1_CHUNK, FFT_N2, 2 * FFT_N2), lambda b, c: (c, 0, 0)),
                  pl.BlockSpec((rows, LANES), lambda b, c: (0, b + gate_block0)),
                  pl.BlockSpec((1, LANES), lambda b, c: (0, b))],
        out_specs=pl.BlockSpec((rows, LANES), lambda b, c: (0, b)),
        out_shape=jax.ShapeDtypeStruct((rows, ch), F32),
        scratch_shapes=[pltpu.VMEM((2, BUF_ROWS, LANES), F32)],
        compiler_params=_cparams(("parallel", "arbitrary")),
        name="long_conv",
    )(z_src, kf, f1_fwd, f1_inv, g, h, gate_src, bias_row)


def _filter_kernel(emb_ref, w1_ref, b1_ref, w2_ref, b2_ref, fr_ref, w3_ref, dl_ref, o_ref, h_ref, *, seq_len):
    i = pl.program_id(0)

    @pl.when((pl.program_id(1) == 0) & (pl.program_id(2) == 0))
    def _():
        fr = fr_ref[...]
        h1 = jnp.sin(fr * (jnp.dot(emb_ref[...], w1_ref[...], precision=HIGHEST, preferred_element_type=F32)
                           + b1_ref[...]))
        h_ref[...] = jnp.sin(fr * (jnp.dot(h1, w2_ref[...], precision=HIGHEST, preferred_element_type=F32)
                                   + b2_ref[...]))

    h2 = h_ref[...]
    h_hi = h2.astype(BF16)
    h_lo = (h2 - h_hi.astype(F32)).astype(BF16)
    w3 = w3_ref[...]
    w_hi = w3.astype(BF16)
    w_lo = (w3 - w_hi.astype(F32)).astype(BF16)
    f = jnp.dot(jnp.concatenate([h_hi, h_hi, h_lo], axis=1), jnp.concatenate([w_hi, w_lo, w_hi], axis=0),
                preferred_element_type=F32)
    t = emb_ref[:, 0:1]
    f = f * jnp.exp(-t * dl_ref[...])
    r = i * FILT_ROWS + lax.broadcasted_iota(jnp.int32, (FILT_ROWS, 1), 0)
    fwd = i < (FFT_N // 2) // FILT_ROWS
    lo = jnp.where(fwd, -1, FFT_N - seq_len)
    hi = jnp.where(fwd, seq_len, FFT_N)
    valid = (r > lo) & (r < hi)
    o_ref[0] = jnp.where(valid, f, 0.0)


def filter_kernels(emb, w1p, b1, w2, b2, freq, w3, deltas, seq_len, tc=512):
    ch = deltas.shape[0]
    hid = w2.shape[0]
    ncb = ch // tc
    half = (FFT_N // 2) // FILT_ROWS
    full = lambda i, o, j: (0, 0)
    return pl.pallas_call(
        functools.partial(_filter_kernel, seq_len=seq_len),
        grid=(FFT_N // FILT_ROWS, HY_ORDERS, ncb),
        in_specs=[pl.BlockSpec((FILT_ROWS, emb.shape[1]), lambda i, o, j: (i, 0)),
                  pl.BlockSpec(w1p.shape, full), pl.BlockSpec((1, hid), full),
                  pl.BlockSpec((hid, hid), full), pl.BlockSpec((1, hid), full), pl.BlockSpec((1, hid), full),
                  pl.BlockSpec((hid, tc), lambda i, o, j: (0, (2 * o + i // half) * ncb + j)),
                  pl.BlockSpec((1, tc), lambda i, o, j: (0, j))],
        out_specs=pl.BlockSpec((1, FILT_ROWS, tc), lambda i, o, j: (o, i, j)),
        out_shape=jax.ShapeDtypeStruct((HY_ORDERS, FFT_N, ch), F32),
        scratch_shapes=[pltpu.VMEM((FILT_ROWS, hid), F32)],
        compiler_params=_cparams(("parallel", "arbitrary", "arbitrary")),
        name="hyena_filters",
    )(emb, w1p, b1.reshape(1, hid), w2, b2.reshape(1, hid), freq.reshape(1, hid), w3, deltas.reshape(1, ch))


def _short_conv_kernel(u_ref, w_ref, o_ref):
    u = u_ref[...]
    rows = u.shape[0]
    r = lax.broadcasted_iota(jnp.int32, (rows, 1), 0)
    prev = jnp.where(r == 0, 0.0, pltpu.roll(u, 1, axis=0))
    nxt = jnp.where(r == rows - 1, 0.0, pltpu.roll(u, rows - 1, axis=0))
    w = w_ref[...]
    o_ref[0:rows, :] = w[0:1, :] * prev + w[1:2, :] * u + w[2:3, :] * nxt
    o_ref[rows:, :] = jnp.zeros((o_ref.shape[0] - rows, o_ref.shape[1]), F32)


def short_conv(p, conv_w, n_cols, col_block0=0):
    rows = p.shape[0]
    return pl.pallas_call(
        _short_conv_kernel,
        grid=(n_cols // LANES,),
        in_specs=[pl.BlockSpec((rows, LANES), lambda j: (0, j + col_block0)),
                  pl.BlockSpec((3, LANES), lambda j: (0, j))],
        out_specs=pl.BlockSpec((SEQ_PAD, LANES), lambda j: (0, j)),
        out_shape=jax.ShapeDtypeStruct((SEQ_PAD, n_cols), F32),
        compiler_params=_cparams(("parallel",)),
        name="short_conv",
    )(p, conv_w)


def position_features(seq_len, n_bands):
    r = np.arange(FFT_N)
    pos = jnp.asarray(np.where(r < FFT_N // 2, r, FFT_N - r).astype(np.float32)[:, None])
    t = pos / (seq_len - 1)
    bands = jnp.linspace(1e-4, n_bands - 1, n_bands, dtype=F32)[None, :]
    ang = bands * (2.0 * math.pi) * pos / seq_len
    emb = jnp.concatenate([t, jnp.cos(ang), -jnp.sin(ang)], axis=-1)
    return jnp.pad(emb, ((0, 0), (0, LANES - emb.shape[1])))


def hyena_branch(p, col_block0, conv_w, w1, b1, w2, b2, freq, w3, bias, deltas, seq_len, n_bands):
    ch = deltas.shape[0]
    assert seq_len % FFT_N2 == 0 and seq_len // FFT_N2 <= SEQ_SLABS and 2 * seq_len - 1 <= FFT_N
    tables = fft_tables()
    emb = position_features(seq_len, n_bands)
    w1p = jnp.pad(w1, ((0, LANES - w1.shape[0]), (0, 0)))
    kern = filter_kernels(emb, w1p, b1, w2, b2, freq, w3, deltas, seq_len, tc=min(512, ch))
    kf = filter_spectrum(kern, tables[0], tables[2])
    uc = short_conv(p, conv_w, 3 * ch, col_block0)
    cb = ch // LANES
    z1 = long_conv(uc, 0, ch, kf, 0, tables, uc, cb, bias[0:1])
    return long_conv(z1, 0, ch, kf, 1, tables, uc, 2 * cb, bias[1:2])


def _mlstm_direction(d, chunk, q_ref, k_ref, v_ref, g_ref, bias_ref, tri_ref, o_ref, c_ref, n_ref, m_ref, seq_len):
    t_idx = lax.broadcasted_iota(jnp.int32, (ML_CHUNK, 1), 0)
    valid = t_idx < (seq_len - chunk * ML_CHUNK)
    g = g_ref[...] + bias_ref[...]
    li_all = jnp.where(valid, g[:, 16 * d:16 * d + 8], NEG_GATE)
    lf_all = jnp.where(valid, jax.nn.log_sigmoid(g[:, 16 * d + 8:16 * d + 16]), 0.0)
    cum_all = jnp.dot(tri_ref[d], lf_all, precision=HIGHEST, preferred_element_type=F32)
    tot_all = jnp.sum(lf_all, axis=0, keepdims=True)
    a_all = tot_all - cum_all + li_all
    amax_all = jnp.max(a_all, axis=0, keepdims=True)
    tt = lax.broadcasted_iota(jnp.int32, (ML_CHUNK, ML_CHUNK), 0)
    ss = lax.broadcasted_iota(jnp.int32, (ML_CHUNK, ML_CHUNK), 1)
    visible = (ss <= tt) if d == 0 else (ss >= tt)
    ones = jnp.ones((ML_CHUNK, 1), F32)
    scale = ML_HEAD_DIM ** -0.5
    for h in range(ML_HEADS):
        sl = slice(h * ML_HEAD_DIM, (h + 1) * ML_HEAD_DIM)
        q = jnp.where(valid, q_ref[:, sl], 0.0)
        k = jnp.where(valid, k_ref[:, sl], 0.0) * scale
        v = jnp.where(valid, v_ref[:, sl], 0.0)
        qb, kb, vb = q.astype(BF16), k.astype(BF16), v.astype(BF16)
        cum = cum_all[:, h:h + 1]
        li = li_all[:, h:h + 1]
        tot = tot_all[:, h:h + 1]
        st = d * ML_HEADS + h
        m_prev = m_ref[st]
        c_prev = c_ref[st]
        n_prev = n_ref[st]
        lhs = jnp.concatenate([cum, ones], axis=1)
        rhs = jnp.concatenate([ones, li - cum], axis=1)
        log_d = lax.dot_general(lhs, rhs, (((1,), (1,)), ((), ())), precision=HIGHEST,
                                preferred_element_type=F32)
        log_d = jnp.where(visible, log_d, -jnp.inf)
        log_inter = cum + m_prev
        m_t = jnp.maximum(log_inter, jnp.max(log_d, axis=1, keepdims=True))
        s = lax.dot_general(qb, kb, (((1,), (1,)), ((), ())), preferred_element_type=F32) * jnp.exp(log_d - m_t)
        w_inter = jnp.exp(log_inter - m_t)
        num = (jnp.dot(s.astype(BF16), vb, preferred_element_type=F32)
               + w_inter * jnp.dot(qb, c_prev.astype(BF16), preferred_element_type=F32))
        den = jnp.sum(s, axis=1, keepdims=True) + w_inter * jnp.sum(q * n_prev, axis=1, keepdims=True)
        o_ref[:, sl] = num / jnp.maximum(jnp.abs(den), jnp.exp(-m_t))
        m_new = jnp.maximum(tot + m_prev, amax_all[:, h:h + 1])
        decay = jnp.exp(tot + m_prev - m_new)
        kw = k * jnp.exp(a_all[:, h:h + 1] - m_new)
        c_ref[st] = decay * c_prev + lax.dot_general(kw.astype(BF16), vb, (((0,), (0,)), ((), ())),
                                                     preferred_element_type=F32)
        n_ref[st] = decay * n_prev + jnp.sum(kw, axis=0, keepdims=True)
        m_ref[st] = m_new


def _mlstm_kernel(qf, kf, vf, gf, qb, kb, vb, gb, bias_ref, tri_ref, of_ref, ob_ref, c_ref, n_ref, m_ref,
                  *, n_chunks, seq_len):
    i = pl.program_id(0)

    @pl.when(i == 0)
    def _():
        c_ref[...] = jnp.zeros(c_ref.shape, F32)
        n_ref[...] = jnp.zeros(n_ref.shape, F32)
        m_ref[...] = jnp.zeros(m_ref.shape, F32)

    _mlstm_direction(0, i, qf, kf, vf, gf, bias_ref, tri_ref, of_ref, c_ref, n_ref, m_ref, seq_len)
    _mlstm_direction(1, n_chunks - 1 - i, qb, kb, vb, gb, bias_ref, tri_ref, ob_ref, c_ref, n_ref, m_ref, seq_len)


def mlstm_scan(qkvo, p_gate, gate_bias):
    seq_len = qkvo.shape[0]
    w = ML_WIDTH
    n_chunks = pl.cdiv(seq_len, ML_CHUNK)
    tri = np.tril(np.ones((ML_CHUNK, ML_CHUNK), np.float32))
    tri = jnp.asarray(np.stack([tri, tri.T]))
    fw = lambda col: (lambda i: (i, col))
    bw = lambda col: (lambda i: (n_chunks - 1 - i, col))
    blk = lambda im: pl.BlockSpec((ML_CHUNK, w), im)
    gblk = lambda im: pl.BlockSpec((ML_CHUNK, 4 * ML_HEADS), im)
    return pl.pallas_call(
        functools.partial(_mlstm_kernel, n_chunks=n_chunks, seq_len=seq_len),
        grid=(n_chunks,),
        in_specs=[blk(fw(0)), blk(fw(1)), blk(fw(2)), gblk(fw(0)),
                  blk(bw(0)), blk(bw(1)), blk(bw(2)), gblk(bw(0)),
                  pl.BlockSpec((1, 4 * ML_HEADS), lambda i: (0, 0)),
                  pl.BlockSpec((2, ML_CHUNK, ML_CHUNK), lambda i: (0, 0, 0))],
        out_specs=[blk(fw(0)), blk(bw(0))],
        out_shape=[jax.ShapeDtypeStruct((seq_len, w), F32), jax.ShapeDtypeStruct((seq_len, w), F32)],
        scratch_shapes=[pltpu.VMEM((2 * ML_HEADS, ML_HEAD_DIM, ML_HEAD_DIM), F32),
                        pltpu.VMEM((2 * ML_HEADS, 1, ML_HEAD_DIM), F32),
                        pltpu.VMEM((2 * ML_HEADS, 1, 1), F32)],
        compiler_params=_cparams(("arbitrary",)),
        name="mlstm_scan",
    )(qkvo, qkvo, qkvo, p_gate, qkvo, qkvo, qkvo, p_gate, gate_bias.reshape(1, -1), tri)


def _mlstm_finish_kernel(hf_ref, hb_ref, o_ref, g_ref, y_ref):
    for h in range(ML_HEADS):
        sl = slice(h * ML_HEAD_DIM, (h + 1) * ML_HEAD_DIM)
        x = hf_ref[:, sl] + hb_ref[:, sl]
        hn = x * lax.rsqrt(jnp.mean(x * x, axis=-1, keepdims=True) + RMS_EPS) * g_ref[:, sl]
        y_ref[:, sl] = (jax.nn.sigmoid(o_ref[:, sl]) * hn).astype(y_ref.dtype)


def mlstm_finish(h_fw, h_bw, qkvo, head_norm, tm, out_dtype):
    seq_len, w = h_fw.shape
    row = lambda i: (i, 0)
    return pl.pallas_call(
        _mlstm_finish_kernel,
        grid=(seq_len // tm,),
        in_specs=[pl.BlockSpec((tm, w), row), pl.BlockSpec((tm, w), row),
                  pl.BlockSpec((tm, w), lambda i: (i, 3)), pl.BlockSpec((1, w), lambda i: (0, 0))],
        out_specs=pl.BlockSpec((tm, w), row),
        out_shape=jax.ShapeDtypeStruct((seq_len, w), out_dtype),
        compiler_params=_cparams(("parallel",)),
        name="mlstm_finish",
    )(h_fw, h_bw, qkvo, head_norm.reshape(1, w))


TOK_TILE = 240
WIN = 256
I32 = jnp.int32
COMBINE_BUFS = 4
GATHER_EXPERTS = 2


def _router_kernel(h_ref, g_ref, w_ref, u_ref, a_ref):
    x = h_ref[...]
    u = (x * lax.rsqrt(jnp.mean(x * x, axis=-1, keepdims=True) + RMS_EPS) * g_ref[...]).astype(BF16)
    u_ref[...] = u
    logits = jnp.dot(u, w_ref[...].astype(BF16), preferred_element_type=F32)
    lane = lax.broadcasted_iota(I32, logits.shape, 1)
    logits = jnp.where(lane < N_EXPERTS, logits, -jnp.inf)
    z = jnp.exp(logits - jnp.max(logits, axis=-1, keepdims=True))
    a_ref[...] = z / jnp.sum(z, axis=-1, keepdims=True)


def router(h, g, w_router, tm):
    m, d = h.shape
    wp = jnp.pad(w_router, ((0, 0), (0, LANES - w_router.shape[1])))
    return pl.pallas_call(
        _router_kernel,
        grid=(m // tm,),
        in_specs=[pl.BlockSpec((tm, d), lambda i: (i, 0)), pl.BlockSpec((1, d), lambda i: (0, 0)),
                  pl.BlockSpec((d, LANES), lambda i: (0, 0))],
        out_specs=[pl.BlockSpec((tm, d), lambda i: (i, 0)), pl.BlockSpec((tm, LANES), lambda i: (i, 0))],
        out_shape=[jax.ShapeDtypeStruct((m, d), BF16), jax.ShapeDtypeStruct((m, LANES), F32)],
        compiler_params=_cparams(("parallel",)),
        name="router",
    )(h, g.reshape(1, d), wp)


def _topk_kernel(a_ref, tri_ref, eye_ref, posc_ref, posr_ref, start_ref, *, cap, n_tiles):
    bits = lax.bitcast_convert_type(a_ref[...], I32)

    def search(b, lo):
        cand = lo | (jnp.int32(1) << (30 - b))
        cnt = jnp.sum((bits >= cand).astype(I32), axis=0, keepdims=True)
        return jnp.where(cnt >= cap, cand, lo)

    thr = lax.fori_loop(0, 31, search, jnp.zeros((1, LANES), I32))
    n_gt = jnp.sum((bits > thr).astype(I32), axis=0, keepdims=True)
    need_eq = (cap - n_gt).astype(F32)
    tri = tri_ref[...]
    eye = eye_ref[...]

    def tile(t, carry):
        eq_before, sel_before = carry
        r0 = pl.multiple_of(t * TOK_TILE, 16)
        b = lax.bitcast_convert_type(a_ref[pl.ds(r0, TOK_TILE), :], I32)
        gt = b > thr
        eq = b == thr
        eq_rank = eq_before + jnp.dot(tri, eq.astype(BF16), preferred_element_type=F32)
        sel = gt | (eq & (eq_rank < need_eq))
        pos = sel_before + jnp.dot(tri, sel.astype(BF16), preferred_element_type=F32)
        posm = jnp.where(sel, pos, -1.0)
        posc_ref[pl.ds(r0, TOK_TILE), :] = posm
        posr_ref[t] = lax.dot_general(eye, posm, (((1,), (1,)), ((), ())), precision=HIGHEST,
                                      preferred_element_type=F32)
        start_ref[pl.ds(t, 1), :] = sel_before
        return (eq_before + jnp.sum(eq.astype(F32), axis=0, keepdims=True),
                sel_before + jnp.sum(sel.astype(F32), axis=0, keepdims=True))

    lax.fori_loop(0, n_tiles, tile, (jnp.zeros((1, LANES), F32), jnp.zeros((1, LANES), F32)))


def expert_choice_topk(aff_pad, cap):
    rows = aff_pad.shape[0]
    n_tiles = rows // TOK_TILE
    tri = jnp.asarray(np.tril(np.ones((TOK_TILE, TOK_TILE), np.float32), -1), BF16)
    eye = jnp.asarray(np.eye(LANES, dtype=np.float32))
    return pl.pallas_call(
        functools.partial(_topk_kernel, cap=cap, n_tiles=n_tiles),
        out_shape=[jax.ShapeDtypeStruct((rows, LANES), F32), jax.ShapeDtypeStruct((n_tiles, LANES, TOK_TILE), F32),
                   jax.ShapeDtypeStruct((n_tiles, LANES), F32)],
        compiler_params=pltpu.CompilerParams(vmem_limit_bytes=VMEM_LIMIT_BYTES),
        name="expert_choice_topk",
    )(aff_pad, tri, eye)


def _gather_kernel(start_ref, u_ref, pr_ref, o_ref, acc_ref, *, seq_len, cap_pad):
    t = pl.program_id(1)

    @pl.when(t == 0)
    def _():
        acc_ref[...] = jnp.zeros(acc_ref.shape, F32)

    row = t * TOK_TILE + lax.broadcasted_iota(I32, (TOK_TILE, 1), 0)
    u = jnp.where(row < seq_len, u_ref[...], 0)
    for i in range(GATHER_EXPERTS):
        e = pl.program_id(0) * GATHER_EXPERTS + i
        ws = pl.multiple_of((start_ref[e, t] // 16) * 16, 16)
        j = (ws + lax.broadcasted_iota(I32, (WIN, 1), 0)).astype(F32)
        onehot = (j == pr_ref[i, 0]).astype(BF16)
        acc_ref[i, pl.ds(ws, WIN), :] += jnp.dot(onehot, u, preferred_element_type=F32)

    @pl.when(t == pl.num_programs(1) - 1)
    def _():
        for i in range(GATHER_EXPERTS):
            o_ref[i] = acc_ref[i, 0:cap_pad, :].astype(o_ref.dtype)


def gather_tokens(u, posm_row, tile_start, cap_pad):
    seq_len, d = u.shape
    n_tiles = posm_row.shape[1]
    grid_spec = pltpu.PrefetchScalarGridSpec(
        num_scalar_prefetch=1,
        grid=(N_EXPERTS // GATHER_EXPERTS, n_tiles),
        in_specs=[pl.BlockSpec((TOK_TILE, d), lambda e, t, s: (t, 0)),
                  pl.BlockSpec((GATHER_EXPERTS, 1, 1, TOK_TILE), lambda e, t, s: (e, t, 0, 0))],
        out_specs=pl.BlockSpec((GATHER_EXPERTS, cap_pad, d), lambda e, t, s: (e, 0, 0)),
        scratch_shapes=[pltpu.VMEM((GATHER_EXPERTS, cap_pad + WIN, d), F32)],
    )
    return pl.pallas_call(
        functools.partial(_gather_kernel, seq_len=seq_len, cap_pad=cap_pad),
        grid_spec=grid_spec,
        out_shape=jax.ShapeDtypeStruct((N_EXPERTS, cap_pad, d), BF16),
        compiler_params=_cparams(("parallel", "arbitrary")),
        name="gather_tokens",
    )(tile_start, u, posm_row)


def _combine_kernel(start_ref, h_ref, pc_ref, a_ref, g_ref, ye_ref, o_ref, buf_ref, sem_ref, *, cap_pad):
    t = pl.program_id(0)
    total = N_EXPERTS * cap_pad

    def window_start(e):
        ws = e * cap_pad + (start_ref[e, t] // 16) * 16
        return pl.multiple_of(jnp.minimum(ws, total - WIN), 16)

    def window_copy(e, slot):
        return pltpu.make_async_copy(ye_ref.at[pl.ds(window_start(e), WIN), :], buf_ref.at[slot], sem_ref.at[slot])

    for e in range(COMBINE_BUFS - 1):
        window_copy(e, e).start()
    acc = h_ref[...]
    posm = pc_ref[...]
    aff = a_ref[...]
    col = lax.broadcasted_iota(I32, (1, WIN), 1).astype(F32)
    for e in range(N_EXPERTS):
        slot = e % COMBINE_BUFS
        ahead = e + COMBINE_BUFS - 1
        if ahead < N_EXPERTS:
            window_copy(ahead, ahead % COMBINE_BUFS).start()
        window_copy(e, slot).wait()
        rel = posm[:, e:e + 1] + (e * cap_pad - window_start(e)).astype(F32)
        onehot = (rel == col).astype(BF16)
        y = jnp.dot(onehot, buf_ref[slot], preferred_element_type=F32)
        acc = acc + aff[:, e:e + 1] * y
    o_ref[...] = acc * lax.rsqrt(jnp.mean(acc * acc, axis=-1, keepdims=True) + RMS_EPS) * g_ref[...]


def moe_combine(h, posm_col, aff_pad, ye, tile_start, g):
    seq_len, d = h.shape
    n_tiles = posm_col.shape[0] // TOK_TILE
    cap_pad = ye.shape[1]
    grid_spec = pltpu.PrefetchScalarGridSpec(
        num_scalar_prefetch=1,
        grid=(n_tiles,),
        in_specs=[pl.BlockSpec((TOK_TILE, d), lambda t, s: (t, 0)),
                  pl.BlockSpec((TOK_TILE, LANES), lambda t, s: (t, 0)),
                  pl.BlockSpec((TOK_TILE, LANES), lambda t, s: (t, 0)),
                  pl.BlockSpec((1, d), lambda t, s: (0, 0)),
                  pl.BlockSpec(memory_space=pl.ANY)],
        out_specs=pl.BlockSpec((TOK_TILE, d), lambda t, s: (t, 0)),
        scratch_shapes=[pltpu.VMEM((COMBINE_BUFS, WIN, d), BF16), pltpu.SemaphoreType.DMA((COMBINE_BUFS,))],
    )
    return pl.pallas_call(
        functools.partial(_combine_kernel, cap_pad=cap_pad),
        grid_spec=grid_spec,
        out_shape=jax.ShapeDtypeStruct((seq_len, d), F32),
        compiler_params=_cparams(("arbitrary",)),
        name="moe_combine",
    )(tile_start, h, posm_col, aff_pad, g.reshape(1, d), ye.reshape(N_EXPERTS * cap_pad, d))


def kernel(x, meta_tokens, norm_mix, w_in, ml_gate_bias, ml_head_norm, hy_conv_w, hy_filt_w1, hy_filt_b1,
           hy_filt_w2, hy_filt_b2, hy_filt_freq, hy_filt_w3, hy_bias, w_branch_a, w_branch_b, w_out,
           norm_ffn, w_router, w_gate, w_up, w_down, norm_final):
    b_ = x.shape[0]
    assert b_ == 1
    h = jnp.concatenate([meta_tokens, x[0]], axis=0)
    seq_len = h.shape[0]
    tm = 912
    assert seq_len % tm == 0
    layer = 0

    w_in_t = jnp.transpose(w_in[layer])
    qkvo = norm_proj(h, norm_mix[layer], w_in_t, OFF_G, 0, tm, 1024)
    w_hm = w_in_t[OFF_HY:].astype(BF16)
    p_hm = norm_proj(h, norm_mix[layer], w_hm, w_hm.shape[0], 0, tm, 1024)
    w_g = jnp.pad(w_in_t[OFF_G:OFF_HY], ((0, 128 - 4 * ML_HEADS), (0, 0)))
    p_g = norm_proj(h, norm_mix[layer], w_g, 128, 0, tm, 128)[:, :4 * ML_HEADS]

    h_fw, h_bw = mlstm_scan(qkvo, p_g, ml_gate_bias[layer])
    y_a = mlstm_finish(h_fw, h_bw, qkvo, ml_head_norm[layer], tm, BF16)
    max_decay = math.log(HY_DECAY_TARGET) / HY_FAST_DECAY_PCT
    min_decay = math.log(HY_DECAY_TARGET) / HY_SLOW_DECAY_PCT
    deltas = jnp.abs(jnp.linspace(min_decay, max_decay, HY_WIDTH, dtype=F32))
    y_b = hyena_branch(p_hm, 0, hy_conv_w[layer], hy_filt_w1[layer], hy_filt_b1[layer], hy_filt_w2[layer],
                       hy_filt_b2[layer], hy_filt_freq[layer], hy_filt_w3[layer], hy_bias[layer], deltas,
                       seq_len, HY_BANDS)

    merged = gated_dual_mm(y_a, y_b, p_hm, 3 * HY_WIDTH, w_branch_a[layer], w_branch_b[layer], tm, 512)
    h = mm_residual(merged, w_out[layer], h, tm, 1024)

    cap = EC_CAPACITY * seq_len // N_EXPERTS
    cap_pad = -(-cap // 16) * 16
    n_tiles = -(-seq_len // TOK_TILE)
    u, aff = router(h, norm_ffn[layer], w_router[layer], tm)
    aff_pad = jnp.pad(aff, ((0, n_tiles * TOK_TILE - seq_len), (0, 0)), constant_values=-1.0)
    posm_col, posm_row, start = expert_choice_topk(aff_pad, cap)
    tile_start = start[:, :N_EXPERTS].T.astype(jnp.int32)
    posm_row = jnp.transpose(posm_row[:, :N_EXPERTS, :], (1, 0, 2))[:, :, None, :]
    xe = gather_tokens(u, posm_row, tile_start, cap_pad)
    ye = expert_ffn(xe, w_gate[layer], w_up[layer], w_down[layer], 256)
    out = moe_combine(h, posm_col, aff_pad, ye, tile_start, norm_final)
    return out[None, N_META:]
```

```python
import functools
import math

import numpy as np
import jax
import jax.numpy as jnp
from jax import lax
from jax.experimental import pallas as pl
from jax.experimental.pallas import tpu as pltpu

D_MODEL = 2048
N_META = 16
ML_HEADS = 8
ML_HEAD_DIM = D_MODEL // ML_HEADS
ML_WIDTH = ML_HEADS * ML_HEAD_DIM
ML_CHUNK = 256
HY_WIDTH = D_MODEL
HY_ORDER = 2
HY_BANDS = 16
HY_FAST_DECAY_PCT = 0.3
HY_SLOW_DECAY_PCT = 1.5
HY_DECAY_TARGET = 1e-2
N_EXPERTS = 16
EC_CAPACITY = 2
D_EXPERT = 5504
RMS_EPS = 1e-6
NEG_GATE = -1e9

OFF_G = 4 * ML_WIDTH
OFF_HY = OFF_G + 4 * ML_HEADS
OFF_MERGE = OFF_HY + 3 * HY_WIDTH

VMEM_LIMIT_BYTES = 56 * 1024 * 1024

F32 = jnp.float32
BF16 = jnp.bfloat16


def _cparams(sem):
    return pltpu.CompilerParams(dimension_semantics=sem, vmem_limit_bytes=VMEM_LIMIT_BYTES)


def _norm_proj_kernel(x_ref, g_ref, w_ref, o_ref, u_ref):
    @pl.when(pl.program_id(1) == 0)
    def _():
        x = x_ref[...]
        y = x * lax.rsqrt(jnp.mean(x * x, axis=-1, keepdims=True) + RMS_EPS)
        u_ref[...] = (y * g_ref[...]).astype(BF16)

    o_ref[...] = lax.dot_general(u_ref[...], w_ref[...].astype(BF16), (((1,), (1,)), ((), ())),
                                 preferred_element_type=F32)


def norm_proj(x, g, wt, n_cols, row_block0, tm, tn):
    m, d = x.shape
    grid = (m // tm, n_cols // tn)
    return pl.pallas_call(
        _norm_proj_kernel,
        grid=grid,
        in_specs=[
            pl.BlockSpec((tm, d), lambda i, j: (i, 0)),
            pl.BlockSpec((1, d), lambda i, j: (0, 0)),
            pl.BlockSpec((tn, d), lambda i, j: (j + row_block0, 0)),
        ],
        out_specs=pl.BlockSpec((tm, tn), lambda i, j: (i, j)),
        out_shape=jax.ShapeDtypeStruct((m, n_cols), F32),
        scratch_shapes=[pltpu.VMEM((tm, d), BF16)],
        compiler_params=_cparams(("parallel", "arbitrary")),
        name="norm_proj",
    )(x, g.reshape(1, d), wt)


def _gated_dual_mm_kernel(ya_ref, yb_ref, pa_ref, pb_ref, wa_ref, wb_ref, o_ref):
    za = jnp.dot(ya_ref[...].astype(BF16), wa_ref[...].astype(BF16), preferred_element_type=F32)
    zb = jnp.dot(yb_ref[...].astype(BF16), wb_ref[...].astype(BF16), preferred_element_type=F32)
    merged = jax.nn.sigmoid(pa_ref[...]) * za + jax.nn.sigmoid(pb_ref[...]) * zb
    o_ref[...] = merged.astype(o_ref.dtype)


def gated_dual_mm(ya, yb, p_merge, gate_col0, wa, wb, tm, tn):
    m, d = ya.shape
    nb = d // tn
    g0 = gate_col0 // tn
    return pl.pallas_call(
        _gated_dual_mm_kernel,
        grid=(m // tm, nb),
        in_specs=[pl.BlockSpec((tm, d), lambda i, j: (i, 0)),
                  pl.BlockSpec((tm, d), lambda i, j: (i, 0)),
                  pl.BlockSpec((tm, tn), lambda i, j: (i, j + g0)),
                  pl.BlockSpec((tm, tn), lambda i, j: (i, j + g0 + nb)),
                  pl.BlockSpec((d, tn), lambda i, j: (0, j)),
                  pl.BlockSpec((d, tn), lambda i, j: (0, j))],
        out_specs=pl.BlockSpec((tm, tn), lambda i, j: (i, j)),
        out_shape=jax.ShapeDtypeStruct((m, d), BF16),
        compiler_params=_cparams(("parallel", "arbitrary")),
        name="gated_dual_mm",
    )(ya, yb, p_merge, p_merge, wa, wb)


def _mm_res_kernel(a_ref, b_ref, r_ref, o_ref):
    o_ref[...] = r_ref[...] + jnp.dot(a_ref[...], b_ref[...].astype(BF16), preferred_element_type=F32)


def mm_residual(a, b, res, tm, tn):
    m, k = a.shape
    _, n = b.shape
    return pl.pallas_call(
        _mm_res_kernel,
        grid=(m // tm, n // tn),
        in_specs=[pl.BlockSpec((tm, k), lambda i, j: (i, 0)),
                  pl.BlockSpec((k, tn), lambda i, j: (0, j)),
                  pl.BlockSpec((tm, tn), lambda i, j: (i, j))],
        out_specs=pl.BlockSpec((tm, tn), lambda i, j: (i, j)),
        out_shape=jax.ShapeDtypeStruct((m, n), F32),
        compiler_params=_cparams(("parallel", "arbitrary")),
        name="mm_residual",
    )(a, b, res)


def _ffn_kernel(xe_ref, wg_ref, wu_ref, wd_ref, o_ref, acc_ref, *, tf, f_total):
    f = pl.program_id(1)

    @pl.when(f == 0)
    def _():
        acc_ref[...] = jnp.zeros(acc_ref.shape, F32)

    x = xe_ref[0]
    a = jnp.dot(x, wg_ref[0].astype(BF16), preferred_element_type=F32)
    b = jnp.dot(x, wu_ref[0].astype(BF16), preferred_element_type=F32)
    hid = (a * jax.nn.sigmoid(a)) * b
    col_ok = (f * tf + lax.broadcasted_iota(jnp.int32, (1, tf), 1)) < f_total
    row_ok = (f * tf + lax.broadcasted_iota(jnp.int32, (tf, 1), 0)) < f_total
    hid = jnp.where(col_ok, hid, 0.0).astype(BF16)
    wd = jnp.where(row_ok, wd_ref[0], 0.0).astype(BF16)
    acc_ref[...] += jnp.dot(hid, wd, preferred_element_type=F32)

    @pl.when(f == pl.num_programs(1) - 1)
    def _():
        o_ref[0] = acc_ref[...].astype(o_ref.dtype)


def expert_ffn(xe, wg, wu, wd, tf):
    e, c, d = xe.shape
    f_total = wg.shape[-1]
    nf = pl.cdiv(f_total, tf)
    return pl.pallas_call(
        functools.partial(_ffn_kernel, tf=tf, f_total=f_total),
        grid=(e, nf),
        in_specs=[pl.BlockSpec((1, c, d), lambda i, f: (i, 0, 0)),
                  pl.BlockSpec((1, d, tf), lambda i, f: (i, 0, f)),
                  pl.BlockSpec((1, d, tf), lambda i, f: (i, 0, f)),
                  pl.BlockSpec((1, tf, d), lambda i, f: (i, f, 0))],
        out_specs=pl.BlockSpec((1, c, d), lambda i, f: (i, 0, 0)),
        out_shape=jax.ShapeDtypeStruct((e, c, d), BF16),
        scratch_shapes=[pltpu.VMEM((c, d), F32)],
        compiler_params=_cparams(("parallel", "arbitrary")),
        name="expert_ffn",
    )(xe, wg, wu, wd)


FFT_N1 = 256
FFT_N2 = 72
FFT_N = FFT_N1 * FFT_N2
SEQ_SLABS = 128
SEQ_PAD = SEQ_SLABS * FFT_N2
LANES = 128
K1_HALF = 144
K1_CHUNK = 24
SPEC_CHUNK = 48
BUF_ROWS = K1_HALF * FFT_N2
K1_UNROLL = 8
N2_UNROLL = 3
HY_ORDERS = 2
FILT_ROWS = 1152
HIGHEST = lax.Precision.HIGHEST


def fft_tables():
    n1 = np.arange(FFT_N1)
    ang1 = 2.0 * np.pi * np.outer(n1, n1) / FFT_N1
    c1, s1 = np.cos(ang1), np.sin(ang1)
    f1_fwd = np.concatenate([c1[:K1_HALF], -s1[:K1_HALF]], axis=0)
    wgt = np.where(np.arange(K1_HALF) > FFT_N1 // 2, 0.0, 2.0)
    wgt[0] = 1.0
    wgt[FFT_N1 // 2] = 1.0
    f1_inv = np.concatenate([c1[:, :K1_HALF] * wgt, -s1[:, :K1_HALF] * wgt], axis=1) / FFT_N
    k2 = np.arange(FFT_N2)
    n2 = np.arange(FFT_N2)
    k1 = np.arange(K1_HALF)
    ang = 2.0 * np.pi * (n2[None, None, :] * (k1[:, None, None] + FFT_N1 * k2[None, :, None])) / FFT_N
    g = np.concatenate([np.cos(ang), -np.sin(ang)], axis=2)
    angt = np.transpose(ang, (0, 2, 1))
    h = np.concatenate([np.cos(angt), -np.sin(angt)], axis=2)
    return (jnp.asarray(f1_fwd, F32), jnp.asarray(f1_inv, F32), jnp.asarray(g, F32), jnp.asarray(h, F32))


def _stage1_fwd(x_ref, f1_ref, buf_ref, n_rows):
    f1 = f1_ref[:, 0:n_rows].astype(BF16)

    def body(j, carry):
        na = 2 * j
        sa = x_ref[pl.ds(na, n_rows, stride=FFT_N2), :]
        sb = x_ref[pl.ds(na + 1, n_rows, stride=FFT_N2), :]
        s = jnp.concatenate([sa, sb], axis=1).astype(BF16)
        u = jnp.dot(f1, s, preferred_element_type=F32)
        buf_ref[0, pl.ds(na, K1_HALF, stride=FFT_N2), :] = u[0:K1_HALF, 0:LANES]
        buf_ref[0, pl.ds(na + 1, K1_HALF, stride=FFT_N2), :] = u[0:K1_HALF, LANES:]
        buf_ref[1, pl.ds(na, K1_HALF, stride=FFT_N2), :] = u[K1_HALF:, 0:LANES]
        buf_ref[1, pl.ds(na + 1, K1_HALF, stride=FFT_N2), :] = u[K1_HALF:, LANES:]
        return carry

    lax.fori_loop(0, FFT_N2 // 2, body, 0, unroll=N2_UNROLL)


def _stage2_fwd(buf_ref, g_ref, row0, kk):
    ar = buf_ref[0, pl.ds(row0, FFT_N2), :]
    ai = buf_ref[1, pl.ds(row0, FFT_N2), :]
    t = jnp.concatenate([jnp.concatenate([ar, ai], axis=1),
                         jnp.concatenate([-ai, ar], axis=1)], axis=0).astype(BF16)
    x = jnp.dot(g_ref[kk].astype(BF16), t, preferred_element_type=F32)
    return x[:, 0:LANES], x[:, LANES:]


def _spectrum_kernel(k_ref, f1_ref, g_ref, o_ref, buf_ref):
    c = pl.program_id(2)

    @pl.when(c == 0)
    def _():
        _stage1_fwd(k_ref, f1_ref, buf_ref, FFT_N1)

    def body(kk, carry):
        row0 = pl.multiple_of((c * SPEC_CHUNK + kk) * FFT_N2, 8)
        xr, xi = _stage2_fwd(buf_ref, g_ref, row0, kk)
        o0 = pl.multiple_of(kk * FFT_N2, 8)
        o_ref[0, pl.ds(o0, FFT_N2), :] = xr
        o_ref[1, pl.ds(o0, FFT_N2), :] = xi
        return carry

    lax.fori_loop(0, SPEC_CHUNK, body, 0, unroll=K1_UNROLL)


def filter_spectrum(kern, f1_fwd, g):
    no, n, ch = kern.shape
    nchunk = K1_HALF // SPEC_CHUNK
    return pl.pallas_call(
        _spectrum_kernel,
        grid=(no, ch // LANES, nchunk),
        in_specs=[pl.BlockSpec((None, n, LANES), lambda o, b, c: (o, 0, b)),
                  pl.BlockSpec((2 * K1_HALF, FFT_N1), lambda o, b, c: (0, 0)),
                  pl.BlockSpec((SPEC_CHUNK, FFT_N2, 2 * FFT_N2), lambda o, b, c: (c, 0, 0))],
        out_specs=pl.BlockSpec((None, 2, SPEC_CHUNK * FFT_N2, LANES), lambda o, b, c: (o, 0, c, b)),
        out_shape=jax.ShapeDtypeStruct((no, 2, BUF_ROWS, ch), F32),
        scratch_shapes=[pltpu.VMEM((2, BUF_ROWS, LANES), F32)],
        compiler_params=_cparams(("parallel", "parallel", "arbitrary")),
        name="filter_spectrum",
    )(kern, f1_fwd, g)


def _conv_kernel(z_ref, kf_ref, f1_ref, f1i_ref, g_ref, h_ref, gate_ref, bias_ref, o_ref, buf_ref):
    c = pl.program_id(1)

    @pl.when(c == 0)
    def _():
        _stage1_fwd(z_ref, f1_ref, buf_ref, SEQ_SLABS)

    def body(kk, carry):
        row0 = pl.multiple_of((c * K1_CHUNK + kk) * FFT_N2, 8)
        xr, xi = _stage2_fwd(buf_ref, g_ref, row0, kk)
        k0 = pl.multiple_of(kk * FFT_N2, 8)
        kr = kf_ref[0, pl.ds(k0, FFT_N2), :]
        ki = kf_ref[1, pl.ds(k0, FFT_N2), :]
        pr = xr * kr - xi * ki
        pi = xr * ki + xi * kr
        t = jnp.concatenate([jnp.concatenate([pr, pi], axis=1),
                             jnp.concatenate([pi, -pr], axis=1)], axis=0).astype(BF16)
        q = jnp.dot(h_ref[kk].astype(BF16), t, preferred_element_type=F32)
        buf_ref[0, pl.ds(row0, FFT_N2), :] = q[:, 0:LANES]
        buf_ref[1, pl.ds(row0, FFT_N2), :] = q[:, LANES:]
        return carry

    lax.fori_loop(0, K1_CHUNK, body, 0, unroll=K1_UNROLL)

    @pl.when(c == pl.num_programs(1) - 1)
    def _():
        f1i = f1i_ref[0:SEQ_SLABS, :].astype(BF16)

        def inv_body(j, carry):
            na = 2 * j
            sra = buf_ref[0, pl.ds(na, K1_HALF, stride=FFT_N2), :]
            srb = buf_ref[0, pl.ds(na + 1, K1_HALF, stride=FFT_N2), :]
            sia = buf_ref[1, pl.ds(na, K1_HALF, stride=FFT_N2), :]
            sib = buf_ref[1, pl.ds(na + 1, K1_HALF, stride=FFT_N2), :]
            s = jnp.concatenate([jnp.concatenate([sra, srb], axis=1),
                                 jnp.concatenate([sia, sib], axis=1)], axis=0).astype(BF16)
            y = jnp.dot(f1i, s, preferred_element_type=F32)
            bias = bias_ref[...]
            for half, n2 in ((0, na), (1, na + 1)):
                rows_n2 = pl.ds(n2, SEQ_SLABS, stride=FFT_N2)
                conv = y[:, half * LANES:(half + 1) * LANES]
                o_ref[rows_n2, :] = gate_ref[rows_n2, :] * (conv + bias * z_ref[rows_n2, :])
            return carry

        lax.fori_loop(0, FFT_N2 // 2, inv_body, 0, unroll=N2_UNROLL)


def long_conv(z_src, z_block0, ch, kf, order, tables, gate_src, gate_block0, bias_row):
    f1_fwd, f1_inv, g, h = tables
    rows = z_src.shape[0]
    assert rows == SEQ_PAD
    nchunk = K1_HALF // K1_CHUNK
    return pl.pallas_call(
        _conv_kernel,
        grid=(ch // LANES, nchunk),
        in_specs=[pl.BlockSpec((rows, LANES), lambda b, c: (0, b + z_block0)),
                  pl.BlockSpec((None, 2, K1_CHUNK * FFT_N2, LANES), lambda b, c: (order, 0, c, b)),
                  pl.BlockSpec((2 * K1_HALF, FFT_N1), lambda b, c: (0, 0)),
                  pl.BlockSpec((FFT_N1, 2 * K1_HALF), lambda b, c: (0, 0)),
                  pl.BlockSpec((K1_CHUNK, FFT_N2, 2 * FFT_N2), lambda b, c: (c, 0, 0)),
                  pl.BlockSpec((K1_CHUNK, FFT_N2, 2 * FFT_N2), lambda b, c: (c, 0, 0)),
                  pl.BlockSpec((rows, LANES), lambda b, c: (0, b + gate_block0)),
                  pl.BlockSpec((1, LANES), lambda b, c: (0, b))],
        out_specs=pl.BlockSpec((rows, LANES), lambda b, c: (0, b)),
        out_shape=jax.ShapeDtypeStruct((rows, ch), F32),
        scratch_shapes=[pltpu.VMEM((2, BUF_ROWS, LANES), F32)],
        compiler_params=_cparams(("parallel", "arbitrary")),
        name="long_conv",
    )(z_src, kf, f1_fwd, f1_inv, g, h, gate_src, bias_row)


def _filter_kernel(emb_ref, w1_ref, b1_ref, w2_ref, b2_ref, fr_ref, w3_ref, dl_ref, o_ref, h_ref, *, seq_len):
    i = pl.program_id(0)

    @pl.when((pl.program_id(1) == 0) & (pl.program_id(2) == 0))
    def _():
        fr = fr_ref[...]
        h1 = jnp.sin(fr * (jnp.dot(emb_ref[...], w1_ref[...], precision=HIGHEST, preferred_element_type=F32)
                           + b1_ref[...]))
        h_ref[...] = jnp.sin(fr * (jnp.dot(h1, w2_ref[...], precision=HIGHEST, preferred_element_type=F32)
                                   + b2_ref[...]))

    h2 = h_ref[...]
    h_hi = h2.astype(BF16)
    h_lo = (h2 - h_hi.astype(F32)).astype(BF16)
    w3 = w3_ref[...]
    w_hi = w3.astype(BF16)
    w_lo = (w3 - w_hi.astype(F32)).astype(BF16)
    f = jnp.dot(jnp.concatenate([h_hi, h_hi, h_lo], axis=1), jnp.concatenate([w_hi, w_lo, w_hi], axis=0),
                preferred_element_type=F32)
    t = emb_ref[:, 0:1]
    f = f * jnp.exp(-t * dl_ref[...])
    r = i * FILT_ROWS + lax.broadcasted_iota(jnp.int32, (FILT_ROWS, 1), 0)
    fwd = i < (FFT_N // 2) // FILT_ROWS
    lo = jnp.where(fwd, -1, FFT_N - seq_len)
    hi = jnp.where(fwd, seq_len, FFT_N)
    valid = (r > lo) & (r < hi)
    o_ref[0] = jnp.where(valid, f, 0.0)


def filter_kernels(emb, w1p, b1, w2, b2, freq, w3, deltas, seq_len, tc=512):
    ch = deltas.shape[0]
    hid = w2.shape[0]
    ncb = ch // tc
    half = (FFT_N // 2) // FILT_ROWS
    full = lambda i, o, j: (0, 0)
    return pl.pallas_call(
        functools.partial(_filter_kernel, seq_len=seq_len),
        grid=(FFT_N // FILT_ROWS, HY_ORDERS, ncb),
        in_specs=[pl.BlockSpec((FILT_ROWS, emb.shape[1]), lambda i, o, j: (i, 0)),
                  pl.BlockSpec(w1p.shape, full), pl.BlockSpec((1, hid), full),
                  pl.BlockSpec((hid, hid), full), pl.BlockSpec((1, hid), full), pl.BlockSpec((1, hid), full),
                  pl.BlockSpec((hid, tc), lambda i, o, j: (0, (2 * o + i // half) * ncb + j)),
                  pl.BlockSpec((1, tc), lambda i, o, j: (0, j))],
        out_specs=pl.BlockSpec((1, FILT_ROWS, tc), lambda i, o, j: (o, i, j)),
        out_shape=jax.ShapeDtypeStruct((HY_ORDERS, FFT_N, ch), F32),
        scratch_shapes=[pltpu.VMEM((FILT_ROWS, hid), F32)],
        compiler_params=_cparams(("parallel", "arbitrary", "arbitrary")),
        name="hyena_filters",
    )(emb, w1p, b1.reshape(1, hid), w2, b2.reshape(1, hid), freq.reshape(1, hid), w3, deltas.reshape(1, ch))


def _short_conv_kernel(u_ref, w_ref, o_ref):
    u = u_ref[...]
    rows = u.shape[0]
    r = lax.broadcasted_iota(jnp.int32, (rows, 1), 0)
    prev = jnp.where(r == 0, 0.0, pltpu.roll(u, 1, axis=0))
    nxt = jnp.where(r == rows - 1, 0.0, pltpu.roll(u, rows - 1, axis=0))
    w = w_ref[...]
    o_ref[0:rows, :] = w[0:1, :] * prev + w[1:2, :] * u + w[2:3, :] * nxt
    o_ref[rows:, :] = jnp.zeros((o_ref.shape[0] - rows, o_ref.shape[1]), F32)


def short_conv(p, conv_w, n_cols, col_block0=0):
    rows = p.shape[0]
    return pl.pallas_call(
        _short_conv_kernel,
        grid=(n_cols // LANES,),
        in_specs=[pl.BlockSpec((rows, LANES), lambda j: (0, j + col_block0)),
                  pl.BlockSpec((3, LANES), lambda j: (0, j))],
        out_specs=pl.BlockSpec((SEQ_PAD, LANES), lambda j: (0, j)),
        out_shape=jax.ShapeDtypeStruct((SEQ_PAD, n_cols), F32),
        compiler_params=_cparams(("parallel",)),
        name="short_conv",
    )(p, conv_w)


def position_features(seq_len, n_bands):
    r = np.arange(FFT_N)
    pos = jnp.asarray(np.where(r < FFT_N // 2, r, FFT_N - r).astype(np.float32)[:, None])
    t = pos / (seq_len - 1)
    bands = jnp.linspace(1e-4, n_bands - 1, n_bands, dtype=F32)[None, :]
    ang = bands * (2.0 * math.pi) * pos / seq_len
    emb = jnp.concatenate([t, jnp.cos(ang), -jnp.sin(ang)], axis=-1)
    return jnp.pad(emb, ((0, 0), (0, LANES - emb.shape[1])))


def hyena_branch(p, col_block0, conv_w, w1, b1, w2, b2, freq, w3, bias, deltas, seq_len, n_bands):
    ch = deltas.shape[0]
    assert seq_len % FFT_N2 == 0 and seq_len // FFT_N2 <= SEQ_SLABS and 2 * seq_len - 1 <= FFT_N
    tables = fft_tables()
    emb = position_features(seq_len, n_bands)
    w1p = jnp.pad(w1, ((0, LANES - w1.shape[0]), (0, 0)))
    kern = filter_kernels(emb, w1p, b1, w2, b2, freq, w3, deltas, seq_len, tc=min(1024, ch))
    kf = filter_spectrum(kern, tables[0], tables[2])
    uc = short_conv(p, conv_w, 3 * ch, col_block0)
    cb = ch // LANES
    z1 = long_conv(uc, 0, ch, kf, 0, tables, uc, cb, bias[0:1])
    return long_conv(z1, 0, ch, kf, 1, tables, uc, 2 * cb, bias[1:2])


def _mlstm_direction(d, chunk, q_ref, k_ref, v_ref, g_ref, bias_ref, tri_ref, o_ref, c_ref, n_ref, m_ref, seq_len):
    t_idx = lax.broadcasted_iota(jnp.int32, (ML_CHUNK, 1), 0)
    valid = t_idx < (seq_len - chunk * ML_CHUNK)
    g = g_ref[...] + bias_ref[...]
    li_all = jnp.where(valid, g[:, 16 * d:16 * d + 8], NEG_GATE)
    lf_all = jnp.where(valid, jax.nn.log_sigmoid(g[:, 16 * d + 8:16 * d + 16]), 0.0)
    cum_all = jnp.dot(tri_ref[d], lf_all, precision=HIGHEST, preferred_element_type=F32)
    tot_all = jnp.sum(lf_all, axis=0, keepdims=True)
    a_all = tot_all - cum_all + li_all
    amax_all = jnp.max(a_all, axis=0, keepdims=True)
    tt = lax.broadcasted_iota(jnp.int32, (ML_CHUNK, ML_CHUNK), 0)
    ss = lax.broadcasted_iota(jnp.int32, (ML_CHUNK, ML_CHUNK), 1)
    visible = (ss <= tt) if d == 0 else (ss >= tt)
    eye = (lax.broadcasted_iota(jnp.int32, (ML_HEADS, ML_HEADS), 0)
           == lax.broadcasted_iota(jnp.int32, (ML_HEADS, ML_HEADS), 1)).astype(F32)
    src_rows = lax.dot_general(eye, li_all - cum_all, (((1,), (1,)), ((), ())), precision=HIGHEST,
                               preferred_element_type=F32)
    scale = ML_HEAD_DIM ** -0.5
    for h in range(ML_HEADS):
        sl = slice(h * ML_HEAD_DIM, (h + 1) * ML_HEAD_DIM)
        q = jnp.where(valid, q_ref[:, sl], 0.0)
        k = jnp.where(valid, k_ref[:, sl], 0.0) * scale
        v = jnp.where(valid, v_ref[:, sl], 0.0)
        qb, kb, vb = q.astype(BF16), k.astype(BF16), v.astype(BF16)
        cum = cum_all[:, h:h + 1]
        li = li_all[:, h:h + 1]
        tot = tot_all[:, h:h + 1]
        st = d * ML_HEADS + h
        m_prev = m_ref[st]
        c_prev = c_ref[st]
        n_prev = n_ref[st]
        log_d = jnp.where(visible, cum + src_rows[h:h + 1, :], -jnp.inf)
        log_inter = cum + m_prev
        m_t = jnp.maximum(log_inter, jnp.max(log_d, axis=1, keepdims=True))
        s = lax.dot_general(qb, kb, (((1,), (1,)), ((), ())), preferred_element_type=F32) * jnp.exp(log_d - m_t)
        w_inter = jnp.exp(log_inter - m_t)
        num = (jnp.dot(s.astype(BF16), vb, preferred_element_type=F32)
               + w_inter * jnp.dot(qb, c_prev.astype(BF16), preferred_element_type=F32))
        den = jnp.sum(s, axis=1, keepdims=True) + w_inter * jnp.sum(q * n_prev, axis=1, keepdims=True)
        o_ref[:, sl] = num / jnp.maximum(jnp.abs(den), jnp.exp(-m_t))
        m_new = jnp.maximum(tot + m_prev, amax_all[:, h:h + 1])
        decay = jnp.exp(tot + m_prev - m_new)
        kw = k * jnp.exp(a_all[:, h:h + 1] - m_new)
        c_ref[st] = decay * c_prev + lax.dot_general(kw.astype(BF16), vb, (((0,), (0,)), ((), ())),
                                                     preferred_element_type=F32)
        n_ref[st] = decay * n_prev + jnp.sum(kw, axis=0, keepdims=True)
        m_ref[st] = m_new


def _mlstm_kernel(qf, kf, vf, gf, qb, kb, vb, gb, bias_ref, tri_ref, of_ref, ob_ref, c_ref, n_ref, m_ref,
                  *, n_chunks, seq_len):
    i = pl.program_id(0)

    @pl.when(i == 0)
    def _():
        c_ref[...] = jnp.zeros(c_ref.shape, F32)
        n_ref[...] = jnp.zeros(n_ref.shape, F32)
        m_ref[...] = jnp.zeros(m_ref.shape, F32)

    _mlstm_direction(0, i, qf, kf, vf, gf, bias_ref, tri_ref, of_ref, c_ref, n_ref, m_ref, seq_len)
    _mlstm_direction(1, n_chunks - 1 - i, qb, kb, vb, gb, bias_ref, tri_ref, ob_ref, c_ref, n_ref, m_ref, seq_len)


def mlstm_scan(qkvo, p_gate, gate_bias):
    seq_len = qkvo.shape[0]
    w = ML_WIDTH
    n_chunks = pl.cdiv(seq_len, ML_CHUNK)
    tri = np.tril(np.ones((ML_CHUNK, ML_CHUNK), np.float32))
    tri = jnp.asarray(np.stack([tri, tri.T]))
    fw = lambda col: (lambda i: (i, col))
    bw = lambda col: (lambda i: (n_chunks - 1 - i, col))
    blk = lambda im: pl.BlockSpec((ML_CHUNK, w), im)
    gblk = lambda im: pl.BlockSpec((ML_CHUNK, 4 * ML_HEADS), im)
    return pl.pallas_call(
        functools.partial(_mlstm_kernel, n_chunks=n_chunks, seq_len=seq_len),
        grid=(n_chunks,),
        in_specs=[blk(fw(0)), blk(fw(1)), blk(fw(2)), gblk(fw(0)),
                  blk(bw(0)), blk(bw(1)), blk(bw(2)), gblk(bw(0)),
                  pl.BlockSpec((1, 4 * ML_HEADS), lambda i: (0, 0)),
                  pl.BlockSpec((2, ML_CHUNK, ML_CHUNK), lambda i: (0, 0, 0))],
        out_specs=[blk(fw(0)), blk(bw(0))],
        out_shape=[jax.ShapeDtypeStruct((seq_len, w), F32), jax.ShapeDtypeStruct((seq_len, w), F32)],
        scratch_shapes=[pltpu.VMEM((2 * ML_HEADS, ML_HEAD_DIM, ML_HEAD_DIM), F32),
                        pltpu.VMEM((2 * ML_HEADS, 1, ML_HEAD_DIM), F32),
                        pltpu.VMEM((2 * ML_HEADS, 1, 1), F32)],
        compiler_params=_cparams(("arbitrary",)),
        name="mlstm_scan",
    )(qkvo, qkvo, qkvo, p_gate, qkvo, qkvo, qkvo, p_gate, gate_bias.reshape(1, -1), tri)


def _mlstm_finish_kernel(hf_ref, hb_ref, o_ref, g_ref, y_ref):
    for h in range(ML_HEADS):
        sl = slice(h * ML_HEAD_DIM, (h + 1) * ML_HEAD_DIM)
        x = hf_ref[:, sl] + hb_ref[:, sl]
        hn = x * lax.rsqrt(jnp.mean(x * x, axis=-1, keepdims=True) + RMS_EPS) * g_ref[:, sl]
        y_ref[:, sl] = (jax.nn.sigmoid(o_ref[:, sl]) * hn).astype(y_ref.dtype)


def mlstm_finish(h_fw, h_bw, qkvo, head_norm, tm, out_dtype):
    seq_len, w = h_fw.shape
    row = lambda i: (i, 0)
    return pl.pallas_call(
        _mlstm_finish_kernel,
        grid=(seq_len // tm,),
        in_specs=[pl.BlockSpec((tm, w), row), pl.BlockSpec((tm, w), row),
                  pl.BlockSpec((tm, w), lambda i: (i, 3)), pl.BlockSpec((1, w), lambda i: (0, 0))],
        out_specs=pl.BlockSpec((tm, w), row),
        out_shape=jax.ShapeDtypeStruct((seq_len, w), out_dtype),
        compiler_params=_cparams(("parallel",)),
        name="mlstm_finish",
    )(h_fw, h_bw, qkvo, head_norm.reshape(1, w))


TOK_TILE = 240
WIN = 256
I32 = jnp.int32
COMBINE_BUFS = 4
GATHER_EXPERTS = 2


def _router_kernel(h_ref, g_ref, w_ref, u_ref, a_ref):
    x = h_ref[...]
    u = (x * lax.rsqrt(jnp.mean(x * x, axis=-1, keepdims=True) + RMS_EPS) * g_ref[...]).astype(BF16)
    u_ref[...] = u
    logits = jnp.dot(u, w_ref[...].astype(BF16), preferred_element_type=F32)
    lane = lax.broadcasted_iota(I32, logits.shape, 1)
    logits = jnp.where(lane < N_EXPERTS, logits, -jnp.inf)
    z = jnp.exp(logits - jnp.max(logits, axis=-1, keepdims=True))
    a_ref[...] = z / jnp.sum(z, axis=-1, keepdims=True)


def router(h, g, w_router, tm):
    m, d = h.shape
    wp = jnp.pad(w_router, ((0, 0), (0, LANES - w_router.shape[1])))
    return pl.pallas_call(
        _router_kernel,
        grid=(m // tm,),
        in_specs=[pl.BlockSpec((tm, d), lambda i: (i, 0)), pl.BlockSpec((1, d), lambda i: (0, 0)),
                  pl.BlockSpec((d, LANES), lambda i: (0, 0))],
        out_specs=[pl.BlockSpec((tm, d), lambda i: (i, 0)), pl.BlockSpec((tm, LANES), lambda i: (i, 0))],
        out_shape=[jax.ShapeDtypeStruct((m, d), BF16), jax.ShapeDtypeStruct((m, LANES), F32)],
        compiler_params=_cparams(("parallel",)),
        name="router",
    )(h, g.reshape(1, d), wp)


def _topk_kernel(a_ref, tri_ref, eye_ref, posc_ref, posr_ref, start_ref, *, cap, n_tiles):
    bits = lax.bitcast_convert_type(a_ref[...], I32)

    def search(b, lo):
        cand = lo | (jnp.int32(1) << (30 - b))
        cnt = jnp.sum((bits >= cand).astype(I32), axis=0, keepdims=True)
        return jnp.where(cnt >= cap, cand, lo)

    thr = lax.fori_loop(0, 31, search, jnp.zeros((1, LANES), I32))
    n_gt = jnp.sum((bits > thr).astype(I32), axis=0, keepdims=True)
    need_eq = (cap - n_gt).astype(F32)
    tri = tri_ref[...]
    eye = eye_ref[...]

    def tile(t, carry):
        eq_before, sel_before = carry
        r0 = pl.multiple_of(t * TOK_TILE, 16)
        b = lax.bitcast_convert_type(a_ref[pl.ds(r0, TOK_TILE), :], I32)
        gt = b > thr
        eq = b == thr
        eq_rank = eq_before + jnp.dot(tri, eq.astype(BF16), preferred_element_type=F32)
        sel = gt | (eq & (eq_rank < need_eq))
        pos = sel_before + jnp.dot(tri, sel.astype(BF16), preferred_element_type=F32)
        posm = jnp.where(sel, pos, -1.0)
        posc_ref[pl.ds(r0, TOK_TILE), :] = posm
        posr_ref[t] = lax.dot_general(eye, posm, (((1,), (1,)), ((), ())), precision=HIGHEST,
                                      preferred_element_type=F32)
        start_ref[pl.ds(t, 1), :] = sel_before
        return (eq_before + jnp.sum(eq.astype(F32), axis=0, keepdims=True),
                sel_before + jnp.sum(sel.astype(F32), axis=0, keepdims=True))

    lax.fori_loop(0, n_tiles, tile, (jnp.zeros((1, LANES), F32), jnp.zeros((1, LANES), F32)))


def expert_choice_topk(aff_pad, cap):
    rows = aff_pad.shape[0]
    n_tiles = rows // TOK_TILE
    tri = jnp.asarray(np.tril(np.ones((TOK_TILE, TOK_TILE), np.float32), -1), BF16)
    eye = jnp.asarray(np.eye(LANES, dtype=np.float32))
    return pl.pallas_call(
        functools.partial(_topk_kernel, cap=cap, n_tiles=n_tiles),
        out_shape=[jax.ShapeDtypeStruct((rows, LANES), F32), jax.ShapeDtypeStruct((n_tiles, LANES, TOK_TILE), F32),
                   jax.ShapeDtypeStruct((n_tiles, LANES), F32)],
        compiler_params=pltpu.CompilerParams(vmem_limit_bytes=VMEM_LIMIT_BYTES),
        name="expert_choice_topk",
    )(aff_pad, tri, eye)


def _gather_kernel(start_ref, u_ref, pr_ref, o_ref, acc_ref, *, seq_len, cap_pad):
    t = pl.program_id(1)

    @pl.when(t == 0)
    def _():
        acc_ref[...] = jnp.zeros(acc_ref.shape, F32)

    row = t * TOK_TILE + lax.broadcasted_iota(I32, (TOK_TILE, 1), 0)
    u = jnp.where(row < seq_len, u_ref[...], 0)
    for i in range(GATHER_EXPERTS):
        e = pl.program_id(0) * GATHER_EXPERTS + i
        ws = pl.multiple_of((start_ref[e, t] // 16) * 16, 16)
        j = (ws + lax.broadcasted_iota(I32, (WIN, 1), 0)).astype(F32)
        onehot = (j == pr_ref[i, 0]).astype(BF16)
        acc_ref[i, pl.ds(ws, WIN), :] += jnp.dot(onehot, u, preferred_element_type=F32)

    @pl.when(t == pl.num_programs(1) - 1)
    def _():
        for i in range(GATHER_EXPERTS):
            o_ref[i] = acc_ref[i, 0:cap_pad, :].astype(o_ref.dtype)


def gather_tokens(u, posm_row, tile_start, cap_pad):
    seq_len, d = u.shape
    n_tiles = posm_row.shape[1]
    grid_spec = pltpu.PrefetchScalarGridSpec(
        num_scalar_prefetch=1,
        grid=(N_EXPERTS // GATHER_EXPERTS, n_tiles),
        in_specs=[pl.BlockSpec((TOK_TILE, d), lambda e, t, s: (t, 0)),
                  pl.BlockSpec((GATHER_EXPERTS, 1, 1, TOK_TILE), lambda e, t, s: (e, t, 0, 0))],
        out_specs=pl.BlockSpec((GATHER_EXPERTS, cap_pad, d), lambda e, t, s: (e, 0, 0)),
        scratch_shapes=[pltpu.VMEM((GATHER_EXPERTS, cap_pad + WIN, d), F32)],
    )
    return pl.pallas_call(
        functools.partial(_gather_kernel, seq_len=seq_len, cap_pad=cap_pad),
        grid_spec=grid_spec,
        out_shape=jax.ShapeDtypeStruct((N_EXPERTS, cap_pad, d), BF16),
        compiler_params=_cparams(("parallel", "arbitrary")),
        name="gather_tokens",
    )(tile_start, u, posm_row)


def _combine_kernel(start_ref, h_ref, pc_ref, a_ref, g_ref, ye_ref, o_ref, buf_ref, sem_ref, *, cap_pad):
    t = pl.program_id(0)
    total = N_EXPERTS * cap_pad

    def window_start(e):
        ws = e * cap_pad + (start_ref[e, t] // 16) * 16
        return pl.multiple_of(jnp.minimum(ws, total - WIN), 16)

    def window_copy(e, slot):
        return pltpu.make_async_copy(ye_ref.at[pl.ds(window_start(e), WIN), :], buf_ref.at[slot], sem_ref.at[slot])

    for e in range(COMBINE_BUFS - 1):
        window_copy(e, e).start()
    acc = h_ref[...]
    posm = pc_ref[...]
    aff = a_ref[...]
    col = lax.broadcasted_iota(I32, (1, WIN), 1).astype(F32)
    for e in range(N_EXPERTS):
        slot = e % COMBINE_BUFS
        ahead = e + COMBINE_BUFS - 1
        if ahead < N_EXPERTS:
            window_copy(ahead, ahead % COMBINE_BUFS).start()
        window_copy(e, slot).wait()
        rel = posm[:, e:e + 1] + (e * cap_pad - window_start(e)).astype(F32)
        onehot = (rel == col).astype(BF16)
        y = jnp.dot(onehot, buf_ref[slot], preferred_element_type=F32)
        acc = acc + aff[:, e:e + 1] * y
    o_ref[...] = acc * lax.rsqrt(jnp.mean(acc * acc, axis=-1, keepdims=True) + RMS_EPS) * g_ref[...]


def moe_combine(h, posm_col, aff_pad, ye, tile_start, g):
    seq_len, d = h.shape
    n_tiles = posm_col.shape[0] // TOK_TILE
    cap_pad = ye.shape[1]
    grid_spec = pltpu.PrefetchScalarGridSpec(
        num_scalar_prefetch=1,
        grid=(n_tiles,),
        in_specs=[pl.BlockSpec((TOK_TILE, d), lambda t, s: (t, 0)),
                  pl.BlockSpec((TOK_TILE, LANES), lambda t, s: (t, 0)),
                  pl.BlockSpec((TOK_TILE, LANES), lambda t, s: (t, 0)),
                  pl.BlockSpec((1, d), lambda t, s: (0, 0)),
                  pl.BlockSpec(memory_space=pl.ANY)],
        out_specs=pl.BlockSpec((TOK_TILE, d), lambda t, s: (t, 0)),
        scratch_shapes=[pltpu.VMEM((COMBINE_BUFS, WIN, d), BF16), pltpu.SemaphoreType.DMA((COMBINE_BUFS,))],
    )
    return pl.pallas_call(
        functools.partial(_combine_kernel, cap_pad=cap_pad),
        grid_spec=grid_spec,
        out_shape=jax.ShapeDtypeStruct((seq_len, d), F32),
        compiler_params=_cparams(("arbitrary",)),
        name="moe_combine",
    )(tile_start, h, posm_col, aff_pad, g.reshape(1, d), ye.reshape(N_EXPERTS * cap_pad, d))


def kernel(x, meta_tokens, norm_mix, w_in, ml_gate_bias, ml_head_norm, hy_conv_w, hy_filt_w1, hy_filt_b1,
           hy_filt_w2, hy_filt_b2, hy_filt_freq, hy_filt_w3, hy_bias, w_branch_a, w_branch_b, w_out,
           norm_ffn, w_router, w_gate, w_up, w_down, norm_final):
    b_ = x.shape[0]
    assert b_ == 1
    h = jnp.concatenate([meta_tokens, x[0]], axis=0)
    seq_len = h.shape[0]
    tm = 912
    assert seq_len % tm == 0
    layer = 0

    w_in_t = jnp.transpose(w_in[layer])
    qkvo = norm_proj(h, norm_mix[layer], w_in_t, OFF_G, 0, tm, 1024)
    w_hm = w_in_t[OFF_HY:].astype(BF16)
    p_hm = norm_proj(h, norm_mix[layer], w_hm, w_hm.shape[0], 0, tm, 1024)
    w_g = jnp.pad(w_in_t[OFF_G:OFF_HY], ((0, 128 - 4 * ML_HEADS), (0, 0)))
    p_g = norm_proj(h, norm_mix[layer], w_g, 128, 0, tm, 128)[:, :4 * ML_HEADS]

    h_fw, h_bw = mlstm_scan(qkvo, p_g, ml_gate_bias[layer])
    y_a = mlstm_finish(h_fw, h_bw, qkvo, ml_head_norm[layer], tm, BF16)
    max_decay = math.log(HY_DECAY_TARGET) / HY_FAST_DECAY_PCT
    min_decay = math.log(HY_DECAY_TARGET) / HY_SLOW_DECAY_PCT
    deltas = jnp.abs(jnp.linspace(min_decay, max_decay, HY_WIDTH, dtype=F32))
    y_b = hyena_branch(p_hm, 0, hy_conv_w[layer], hy_filt_w1[layer], hy_filt_b1[layer], hy_filt_w2[layer],
                       hy_filt_b2[layer], hy_filt_freq[layer], hy_filt_w3[layer], hy_bias[layer], deltas,
                       seq_len, HY_BANDS)

    merged = gated_dual_mm(y_a, y_b, p_hm, 3 * HY_WIDTH, w_branch_a[layer], w_branch_b[layer], tm, 512)
    h = mm_residual(merged, w_out[layer], h, tm, 1024)

    cap = EC_CAPACITY * seq_len // N_EXPERTS
    cap_pad = -(-cap // 16) * 16
    n_tiles = -(-seq_len // TOK_TILE)
    u, aff = router(h, norm_ffn[layer], w_router[layer], tm)
    aff_pad = jnp.pad(aff, ((0, n_tiles * TOK_TILE - seq_len), (0, 0)), constant_values=-1.0)
    posm_col, posm_row, start = expert_choice_topk(aff_pad, cap)
    tile_start = start[:, :N_EXPERTS].T.astype(jnp.int32)
    posm_row = jnp.transpose(posm_row[:, :N_EXPERTS, :], (1, 0, 2))[:, :, None, :]
    xe = gather_tokens(u, posm_row, tile_start, cap_pad)
    ye = expert_ffn(xe, w_gate[layer], w_up[layer], w_down[layer], 256)
    out = moe_combine(h, posm_col, aff_pad, ye, tile_start, norm_final)
    return out[None, N_META:]
```

```python
import functools
import math

import numpy as np
import jax
import jax.numpy as jnp
from jax import lax
from jax.experimental import pallas as pl
from jax.experimental.pallas import tpu as pltpu

D_MODEL = 2048
N_META = 16
ML_HEADS = 8
ML_HEAD_DIM = D_MODEL // ML_HEADS
ML_WIDTH = ML_HEADS * ML_HEAD_DIM
ML_CHUNK = 256
HY_WIDTH = D_MODEL
HY_ORDER = 2
HY_BANDS = 16
HY_FAST_DECAY_PCT = 0.3
HY_SLOW_DECAY_PCT = 1.5
HY_DECAY_TARGET = 1e-2
N_EXPERTS = 16
EC_CAPACITY = 2
D_EXPERT = 5504
RMS_EPS = 1e-6
NEG_GATE = -1e9

OFF_G = 4 * ML_WIDTH
OFF_HY = OFF_G + 4 * ML_HEADS
OFF_MERGE = OFF_HY + 3 * HY_WIDTH

VMEM_LIMIT_BYTES = 56 * 1024 * 1024

F32 = jnp.float32
BF16 = jnp.bfloat16


def _cparams(sem):
    return pltpu.CompilerParams(dimension_semantics=sem, vmem_limit_bytes=VMEM_LIMIT_BYTES)


def _norm_proj_kernel(x_ref, g_ref, w_ref, o_ref, u_ref):
    @pl.when(pl.program_id(1) == 0)
    def _():
        x = x_ref[...]
        y = x * lax.rsqrt(jnp.mean(x * x, axis=-1, keepdims=True) + RMS_EPS)
        u_ref[...] = (y * g_ref[...]).astype(BF16)

    o_ref[...] = lax.dot_general(u_ref[...], w_ref[...].astype(BF16), (((1,), (1,)), ((), ())),
                                 preferred_element_type=F32)


def norm_proj(x, g, wt, n_cols, row_block0, tm, tn):
    m, d = x.shape
    grid = (m // tm, n_cols // tn)
    return pl.pallas_call(
        _norm_proj_kernel,
        grid=grid,
        in_specs=[
            pl.BlockSpec((tm, d), lambda i, j: (i, 0)),
            pl.BlockSpec((1, d), lambda i, j: (0, 0)),
            pl.BlockSpec((tn, d), lambda i, j: (j + row_block0, 0)),
        ],
        out_specs=pl.BlockSpec((tm, tn), lambda i, j: (i, j)),
        out_shape=jax.ShapeDtypeStruct((m, n_cols), F32),
        scratch_shapes=[pltpu.VMEM((tm, d), BF16)],
        compiler_params=_cparams(("parallel", "arbitrary")),
        name="norm_proj",
    )(x, g.reshape(1, d), wt)


def _gated_dual_mm_kernel(ya_ref, yb_ref, pa_ref, pb_ref, wa_ref, wb_ref, o_ref):
    za = jnp.dot(ya_ref[...].astype(BF16), wa_ref[...].astype(BF16), preferred_element_type=F32)
    zb = jnp.dot(yb_ref[...].astype(BF16), wb_ref[...].astype(BF16), preferred_element_type=F32)
    merged = jax.nn.sigmoid(pa_ref[...]) * za + jax.nn.sigmoid(pb_ref[...]) * zb
    o_ref[...] = merged.astype(o_ref.dtype)


def gated_dual_mm(ya, yb, p_merge, gate_col0, wa, wb, tm, tn):
    m, d = ya.shape
    nb = d // tn
    g0 = gate_col0 // tn
    return pl.pallas_call(
        _gated_dual_mm_kernel,
        grid=(m // tm, nb),
        in_specs=[pl.BlockSpec((tm, d), lambda i, j: (i, 0)),
                  pl.BlockSpec((tm, d), lambda i, j: (i, 0)),
                  pl.BlockSpec((tm, tn), lambda i, j: (i, j + g0)),
                  pl.BlockSpec((tm, tn), lambda i, j: (i, j + g0 + nb)),
                  pl.BlockSpec((d, tn), lambda i, j: (0, j)),
                  pl.BlockSpec((d, tn), lambda i, j: (0, j))],
        out_specs=pl.BlockSpec((tm, tn), lambda i, j: (i, j)),
        out_shape=jax.ShapeDtypeStruct((m, d), BF16),
        compiler_params=_cparams(("parallel", "arbitrary")),
        name="gated_dual_mm",
    )(ya, yb, p_merge, p_merge, wa, wb)


def _mm_res_kernel(a_ref, b_ref, r_ref, o_ref):
    o_ref[...] = r_ref[...] + jnp.dot(a_ref[...], b_ref[...].astype(BF16), preferred_element_type=F32)


def mm_residual(a, b, res, tm, tn):
    m, k = a.shape
    _, n = b.shape
    return pl.pallas_call(
        _mm_res_kernel,
        grid=(m // tm, n // tn),
        in_specs=[pl.BlockSpec((tm, k), lambda i, j: (i, 0)),
                  pl.BlockSpec((k, tn), lambda i, j: (0, j)),
                  pl.BlockSpec((tm, tn), lambda i, j: (i, j))],
        out_specs=pl.BlockSpec((tm, tn), lambda i, j: (i, j)),
        out_shape=jax.ShapeDtypeStruct((m, n), F32),
        compiler_params=_cparams(("parallel", "arbitrary")),
        name="mm_residual",
    )(a, b, res)


def _ffn_kernel(xe_ref, wg_ref, wu_ref, wd_ref, o_ref, acc_ref, *, tf, f_total):
    f = pl.program_id(1)

    @pl.when(f == 0)
    def _():
        acc_ref[...] = jnp.zeros(acc_ref.shape, F32)

    x = xe_ref[0]
    a = jnp.dot(x, wg_ref[0].astype(BF16), preferred_element_type=F32)
    b = jnp.dot(x, wu_ref[0].astype(BF16), preferred_element_type=F32)
    hid = (a * jax.nn.sigmoid(a)) * b
    col_ok = (f * tf + lax.broadcasted_iota(jnp.int32, (1, tf), 1)) < f_total
    row_ok = (f * tf + lax.broadcasted_iota(jnp.int32, (tf, 1), 0)) < f_total
    hid = jnp.where(col_ok, hid, 0.0).astype(BF16)
    wd = jnp.where(row_ok, wd_ref[0], 0.0).astype(BF16)
    acc_ref[...] += jnp.dot(hid, wd, preferred_element_type=F32)

    @pl.when(f == pl.num_programs(1) - 1)
    def _():
        o_ref[0] = acc_ref[...].astype(o_ref.dtype)


def expert_ffn(xe, wg, wu, wd, tf):
    e, c, d = xe.shape
    f_total = wg.shape[-1]
    nf = pl.cdiv(f_total, tf)
    return pl.pallas_call(
        functools.partial(_ffn_kernel, tf=tf, f_total=f_total),
        grid=(e, nf),
        in_specs=[pl.BlockSpec((1, c, d), lambda i, f: (i, 0, 0)),
                  pl.BlockSpec((1, d, tf), lambda i, f: (i, 0, f)),
                  pl.BlockSpec((1, d, tf), lambda i, f: (i, 0, f)),
                  pl.BlockSpec((1, tf, d), lambda i, f: (i, f, 0))],
        out_specs=pl.BlockSpec((1, c, d), lambda i, f: (i, 0, 0)),
        out_shape=jax.ShapeDtypeStruct((e, c, d), BF16),
        scratch_shapes=[pltpu.VMEM((c, d), F32)],
        compiler_params=_cparams(("parallel", "arbitrary")),
        name="expert_ffn",
    )(xe, wg, wu, wd)


FFT_N1 = 256
FFT_N2 = 72
FFT_N = FFT_N1 * FFT_N2
SEQ_SLABS = 128
SEQ_PAD = SEQ_SLABS * FFT_N2
LANES = 128
K1_HALF = 144
K1_CHUNK = 24
SPEC_CHUNK = 48
BUF_ROWS = K1_HALF * FFT_N2
K1_UNROLL = 24
N2_UNROLL = 6
HY_ORDERS = 2
FILT_ROWS = 1152
HIGHEST = lax.Precision.HIGHEST


def fft_tables():
    n1 = np.arange(FFT_N1)
    ang1 = 2.0 * np.pi * np.outer(n1, n1) / FFT_N1
    c1, s1 = np.cos(ang1), np.sin(ang1)
    f1_fwd = np.concatenate([c1[:K1_HALF], -s1[:K1_HALF]], axis=0)
    wgt = np.where(np.arange(K1_HALF) > FFT_N1 // 2, 0.0, 2.0)
    wgt[0] = 1.0
    wgt[FFT_N1 // 2] = 1.0
    f1_inv = np.concatenate([c1[:, :K1_HALF] * wgt, -s1[:, :K1_HALF] * wgt], axis=1) / FFT_N
    k2 = np.arange(FFT_N2)
    n2 = np.arange(FFT_N2)
    k1 = np.arange(K1_HALF)
    ang = 2.0 * np.pi * (n2[None, None, :] * (k1[:, None, None] + FFT_N1 * k2[None, :, None])) / FFT_N
    g = np.concatenate([np.cos(ang), -np.sin(ang)], axis=2)
    angt = np.transpose(ang, (0, 2, 1))
    h = np.concatenate([np.cos(angt), -np.sin(angt)], axis=2)
    return (jnp.asarray(f1_fwd, F32), jnp.asarray(f1_inv, F32), jnp.asarray(g, F32), jnp.asarray(h, F32))


def _stage1_fwd(x_ref, f1_ref, buf_ref, n_rows):
    f1 = f1_ref[:, 0:n_rows].astype(BF16)

    def body(j, carry):
        na = 2 * j
        sa = x_ref[pl.ds(na, n_rows, stride=FFT_N2), :]
        sb = x_ref[pl.ds(na + 1, n_rows, stride=FFT_N2), :]
        s = jnp.concatenate([sa, sb], axis=1).astype(BF16)
        u = jnp.dot(f1, s, preferred_element_type=F32)
        buf_ref[0, pl.ds(na, K1_HALF, stride=FFT_N2), :] = u[0:K1_HALF, 0:LANES]
        buf_ref[0, pl.ds(na + 1, K1_HALF, stride=FFT_N2), :] = u[0:K1_HALF, LANES:]
        buf_ref[1, pl.ds(na, K1_HALF, stride=FFT_N2), :] = u[K1_HALF:, 0:LANES]
        buf_ref[1, pl.ds(na + 1, K1_HALF, stride=FFT_N2), :] = u[K1_HALF:, LANES:]
        return carry

    lax.fori_loop(0, FFT_N2 // 2, body, 0, unroll=N2_UNROLL)


def _stage2_fwd(buf_ref, g_ref, row0, kk):
    ar = buf_ref[0, pl.ds(row0, FFT_N2), :]
    ai = buf_ref[1, pl.ds(row0, FFT_N2), :]
    t = jnp.concatenate([jnp.concatenate([ar, ai], axis=1),
                         jnp.concatenate([-ai, ar], axis=1)], axis=0).astype(BF16)
    x = jnp.dot(g_ref[kk].astype(BF16), t, preferred_element_type=F32)
    return x[:, 0:LANES], x[:, LANES:]


def _spectrum_kernel(k_ref, f1_ref, g_ref, o_ref, buf_ref):
    c = pl.program_id(2)

    @pl.when(c == 0)
    def _():
        _stage1_fwd(k_ref, f1_ref, buf_ref, FFT_N1)

    def body(kk, carry):
        row0 = pl.multiple_of((c * SPEC_CHUNK + kk) * FFT_N2, 8)
        xr, xi = _stage2_fwd(buf_ref, g_ref, row0, kk)
        o0 = pl.multiple_of(kk * FFT_N2, 8)
        o_ref[0, pl.ds(o0, FFT_N2), :] = xr
        o_ref[1, pl.ds(o0, FFT_N2), :] = xi
        return carry

    lax.fori_loop(0, SPEC_CHUNK, body, 0, unroll=K1_UNROLL)


def filter_spectrum(kern, f1_fwd, g):
    no, n, ch = kern.shape
    nchunk = K1_HALF // SPEC_CHUNK
    return pl.pallas_call(
        _spectrum_kernel,
        grid=(no, ch // LANES, nchunk),
        in_specs=[pl.BlockSpec((None, n, LANES), lambda o, b, c: (o, 0, b)),
                  pl.BlockSpec((2 * K1_HALF, FFT_N1), lambda o, b, c: (0, 0)),
                  pl.BlockSpec((SPEC_CHUNK, FFT_N2, 2 * FFT_N2), lambda o, b, c: (c, 0, 0))],
        out_specs=pl.BlockSpec((None, 2, SPEC_CHUNK * FFT_N2, LANES), lambda o, b, c: (o, 0, c, b)),
        out_shape=jax.ShapeDtypeStruct((no, 2, BUF_ROWS, ch), F32),
        scratch_shapes=[pltpu.VMEM((2, BUF_ROWS, LANES), F32)],
        compiler_params=_cparams(("parallel", "parallel", "arbitrary")),
        name="filter_spectrum",
    )(kern, f1_fwd, g)


def _conv_kernel(z_ref, kf_ref, f1_ref, f1i_ref, g_ref, h_ref, gate_ref, bias_ref, o_ref, buf_ref):
    c = pl.program_id(1)

    @pl.when(c == 0)
    def _():
        _stage1_fwd(z_ref, f1_ref, buf_ref, SEQ_SLABS)

    def body(kk, carry):
        row0 = pl.multiple_of((c * K1_CHUNK + kk) * FFT_N2, 8)
        xr, xi = _stage2_fwd(buf_ref, g_ref, row0, kk)
        k0 = pl.multiple_of(kk * FFT_N2, 8)
        kr = kf_ref[0, pl.ds(k0, FFT_N2), :]
        ki = kf_ref[1, pl.ds(k0, FFT_N2), :]
        pr = xr * kr - xi * ki
        pi = xr * ki + xi * kr
        t = jnp.concatenate([jnp.concatenate([pr, pi], axis=1),
                             jnp.concatenate([pi, -pr], axis=1)], axis=0).astype(BF16)
        q = jnp.dot(h_ref[kk].astype(BF16), t, preferred_element_type=F32)
        buf_ref[0, pl.ds(row0, FFT_N2), :] = q[:, 0:LANES]
        buf_ref[1, pl.ds(row0, FFT_N2), :] = q[:, LANES:]
        return carry

    lax.fori_loop(0, K1_CHUNK, body, 0, unroll=K1_UNROLL)

    @pl.when(c == pl.num_programs(1) - 1)
    def _():
        f1i = f1i_ref[0:SEQ_SLABS, :].astype(BF16)

        def inv_body(j, carry):
            na = 2 * j
            sra = buf_ref[0, pl.ds(na, K1_HALF, stride=FFT_N2), :]
            srb = buf_ref[0, pl.ds(na + 1, K1_HALF, stride=FFT_N2), :]
            sia = buf_ref[1, pl.ds(na, K1_HALF, stride=FFT_N2), :]
            sib = buf_ref[1, pl.ds(na + 1, K1_HALF, stride=FFT_N2), :]
            s = jnp.concatenate([jnp.concatenate([sra, srb], axis=1),
                                 jnp.concatenate([sia, sib], axis=1)], axis=0).astype(BF16)
            y = jnp.dot(f1i, s, preferred_element_type=F32)
            bias = bias_ref[...]
            for half, n2 in ((0, na), (1, na + 1)):
                rows_n2 = pl.ds(n2, SEQ_SLABS, stride=FFT_N2)
                conv = y[:, half * LANES:(half + 1) * LANES]
                o_ref[rows_n2, :] = gate_ref[rows_n2, :] * (conv + bias * z_ref[rows_n2, :])
            return carry

        lax.fori_loop(0, FFT_N2 // 2, inv_body, 0, unroll=N2_UNROLL)


def long_conv(z_src, z_block0, ch, kf, order, tables, gate_src, gate_block0, bias_row):
    f1_fwd, f1_inv, g, h = tables
    rows = z_src.shape[0]
    assert rows == SEQ_PAD
    nchunk = K1_HALF // K1_CHUNK
    return pl.pallas_call(
        _conv_kernel,
        grid=(ch // LANES, nchunk),
        in_specs=[pl.BlockSpec((rows, LANES), lambda b, c: (0, b + z_block0)),
                  pl.BlockSpec((None, 2, K1_CHUNK * FFT_N2, LANES), lambda b, c: (order, 0, c, b)),
                  pl.BlockSpec((2 * K1_HALF, FFT_N1), lambda b, c: (0, 0)),
                  pl.BlockSpec((FFT_N1, 2 * K1_HALF), lambda b, c: (0, 0)),
                  pl.BlockSpec((K1_CHUNK, FFT_N2, 2 * FFT_N2), lambda b, c: (c, 0, 0)),
                  pl.BlockSpec((K1_CHUNK, FFT_N2, 2 * FFT_N2), lambda b, c: (c, 0, 0)),
                  pl.BlockSpec((rows, LANES), lambda b, c: (0, b + gate_block0)),
                  pl.BlockSpec((1, LANES), lambda b, c: (0, b))],
        out_specs=pl.BlockSpec((rows, LANES), lambda b, c: (0, b)),
        out_shape=jax.ShapeDtypeStruct((rows, ch), F32),
        scratch_shapes=[pltpu.VMEM((2, BUF_ROWS, LANES), F32)],
        compiler_params=_cparams(("parallel", "arbitrary")),
        name="long_conv",
    )(z_src, kf, f1_fwd, f1_inv, g, h, gate_src, bias_row)


def _filter_kernel(emb_ref, w1_ref, b1_ref, w2_ref, b2_ref, fr_ref, w3_ref, dl_ref, o_ref, h_ref, *, seq_len):
    i = pl.program_id(0)

    @pl.when((pl.program_id(1) == 0) & (pl.program_id(2) == 0))
    def _():
        fr = fr_ref[...]
        h1 = jnp.sin(fr * (jnp.dot(emb_ref[...], w1_ref[...], precision=HIGHEST, preferred_element_type=F32)
                           + b1_ref[...]))
        h_ref[...] = jnp.sin(fr * (jnp.dot(h1, w2_ref[...], precision=HIGHEST, preferred_element_type=F32)
                                   + b2_ref[...]))

    h2 = h_ref[...]
    h_hi = h2.astype(BF16)
    h_lo = (h2 - h_hi.astype(F32)).astype(BF16)
    w3 = w3_ref[...]
    w_hi = w3.astype(BF16)
    w_lo = (w3 - w_hi.astype(F32)).astype(BF16)
    f = jnp.dot(jnp.concatenate([h_hi, h_hi, h_lo], axis=1), jnp.concatenate([w_hi, w_lo, w_hi], axis=0),
                preferred_element_type=F32)
    t = emb_ref[:, 0:1]
    f = f * jnp.exp(-t * dl_ref[...])
    r = i * FILT_ROWS + lax.broadcasted_iota(jnp.int32, (FILT_ROWS, 1), 0)
    fwd = i < (FFT_N // 2) // FILT_ROWS
    lo = jnp.where(fwd, -1, FFT_N - seq_len)
    hi = jnp.where(fwd, seq_len, FFT_N)
    valid = (r > lo) & (r < hi)
    o_ref[0] = jnp.where(valid, f, 0.0)


def filter_kernels(emb, w1p, b1, w2, b2, freq, w3, deltas, seq_len, tc=512):
    ch = deltas.shape[0]
    hid = w2.shape[0]
    ncb = ch // tc
    half = (FFT_N // 2) // FILT_ROWS
    full = lambda i, o, j: (0, 0)
    return pl.pallas_call(
        functools.partial(_filter_kernel, seq_len=seq_len),
        grid=(FFT_N // FILT_ROWS, HY_ORDERS, ncb),
        in_specs=[pl.BlockSpec((FILT_ROWS, emb.shape[1]), lambda i, o, j: (i, 0)),
                  pl.BlockSpec(w1p.shape, full), pl.BlockSpec((1, hid), full),
                  pl.BlockSpec((hid, hid), full), pl.BlockSpec((1, hid), full), pl.BlockSpec((1, hid), full),
                  pl.BlockSpec((hid, tc), lambda i, o, j: (0, (2 * o + i // half) * ncb + j)),
                  pl.BlockSpec((1, tc), lambda i, o, j: (0, j))],
        out_specs=pl.BlockSpec((1, FILT_ROWS, tc), lambda i, o, j: (o, i, j)),
        out_shape=jax.ShapeDtypeStruct((HY_ORDERS, FFT_N, ch), F32),
        scratch_shapes=[pltpu.VMEM((FILT_ROWS, hid), F32)],
        compiler_params=_cparams(("parallel", "arbitrary", "arbitrary")),
        name="hyena_filters",
    )(emb, w1p, b1.reshape(1, hid), w2, b2.reshape(1, hid), freq.reshape(1, hid), w3, deltas.reshape(1, ch))


def _short_conv_kernel(u_ref, w_ref, o_ref):
    u = u_ref[...]
    rows = u.shape[0]
    r = lax.broadcasted_iota(jnp.int32, (rows, 1), 0)
    prev = jnp.where(r == 0, 0.0, pltpu.roll(u, 1, axis=0))
    nxt = jnp.where(r == rows - 1, 0.0, pltpu.roll(u, rows - 1, axis=0))
    w = w_ref[...]
    o_ref[0:rows, :] = w[0:1, :] * prev + w[1:2, :] * u + w[2:3, :] * nxt
    o_ref[rows:, :] = jnp.zeros((o_ref.shape[0] - rows, o_ref.shape[1]), F32)


def short_conv(p, conv_w, n_cols, col_block0=0):
    rows = p.shape[0]
    return pl.pallas_call(
        _short_conv_kernel,
        grid=(n_cols // LANES,),
        in_specs=[pl.BlockSpec((rows, LANES), lambda j: (0, j + col_block0)),
                  pl.BlockSpec((3, LANES), lambda j: (0, j))],
        out_specs=pl.BlockSpec((SEQ_PAD, LANES), lambda j: (0, j)),
        out_shape=jax.ShapeDtypeStruct((SEQ_PAD, n_cols), F32),
        compiler_params=_cparams(("parallel",)),
        name="short_conv",
    )(p, conv_w)


def position_features(seq_len, n_bands):
    r = np.arange(FFT_N)
    pos = jnp.asarray(np.where(r < FFT_N // 2, r, FFT_N - r).astype(np.float32)[:, None])
    t = pos / (seq_len - 1)
    bands = jnp.linspace(1e-4, n_bands - 1, n_bands, dtype=F32)[None, :]
    ang = bands * (2.0 * math.pi) * pos / seq_len
    emb = jnp.concatenate([t, jnp.cos(ang), -jnp.sin(ang)], axis=-1)
    return jnp.pad(emb, ((0, 0), (0, LANES - emb.shape[1])))


def hyena_branch(p, col_block0, conv_w, w1, b1, w2, b2, freq, w3, bias, deltas, seq_len, n_bands):
    ch = deltas.shape[0]
    assert seq_len % FFT_N2 == 0 and seq_len // FFT_N2 <= SEQ_SLABS and 2 * seq_len - 1 <= FFT_N
    tables = fft_tables()
    emb = position_features(seq_len, n_bands)
    w1p = jnp.pad(w1, ((0, LANES - w1.shape[0]), (0, 0)))
    kern = filter_kernels(emb, w1p, b1, w2, b2, freq, w3, deltas, seq_len, tc=min(1024, ch))
    kf = filter_spectrum(kern, tables[0], tables[2])
    uc = short_conv(p, conv_w, 3 * ch, col_block0)
    cb = ch // LANES
    z1 = long_conv(uc, 0, ch, kf, 0, tables, uc, cb, bias[0:1])
    return long_conv(z1, 0, ch, kf, 1, tables, uc, 2 * cb, bias[1:2])


def _mlstm_direction(d, chunk, q_ref, k_ref, v_ref, g_ref, bias_ref, tri_ref, o_ref, c_ref, n_ref, m_ref, seq_len):
    t_idx = lax.broadcasted_iota(jnp.int32, (ML_CHUNK, 1), 0)
    valid = t_idx < (seq_len - chunk * ML_CHUNK)
    g = g_ref[...] + bias_ref[...]
    li_all = jnp.where(valid, g[:, 16 * d:16 * d + 8], NEG_GATE)
    lf_all = jnp.where(valid, jax.nn.log_sigmoid(g[:, 16 * d + 8:16 * d + 16]), 0.0)
    cum_all = jnp.dot(tri_ref[d], lf_all, precision=HIGHEST, preferred_element_type=F32)
    tot_all = jnp.sum(lf_all, axis=0, keepdims=True)
    a_all = tot_all - cum_all + li_all
    amax_all = jnp.max(a_all, axis=0, keepdims=True)
    tt = lax.broadcasted_iota(jnp.int32, (ML_CHUNK, ML_CHUNK), 0)
    ss = lax.broadcasted_iota(jnp.int32, (ML_CHUNK, ML_CHUNK), 1)
    visible = (ss <= tt) if d == 0 else (ss >= tt)
    eye = (lax.broadcasted_iota(jnp.int32, (ML_HEADS, ML_HEADS), 0)
           == lax.broadcasted_iota(jnp.int32, (ML_HEADS, ML_HEADS), 1)).astype(F32)
    src_rows = lax.dot_general(eye, li_all - cum_all, (((1,), (1,)), ((), ())), precision=HIGHEST,
                               preferred_element_type=F32)
    scale = ML_HEAD_DIM ** -0.5
    for h in range(ML_HEADS):
        sl = slice(h * ML_HEAD_DIM, (h + 1) * ML_HEAD_DIM)
        q = jnp.where(valid, q_ref[:, sl], 0.0)
        k = jnp.where(valid, k_ref[:, sl], 0.0) * scale
        v = jnp.where(valid, v_ref[:, sl], 0.0)
        qb, kb, vb = q.astype(BF16), k.astype(BF16), v.astype(BF16)
        cum = cum_all[:, h:h + 1]
        li = li_all[:, h:h + 1]
        tot = tot_all[:, h:h + 1]
        st = d * ML_HEADS + h
        m_prev = m_ref[st]
        c_prev = c_ref[st]
        n_prev = n_ref[st]
        log_d = jnp.where(visible, cum + src_rows[h:h + 1, :], -jnp.inf)
        log_inter = cum + m_prev
        m_t = jnp.maximum(log_inter, jnp.max(log_d, axis=1, keepdims=True))
        s = lax.dot_general(qb, kb, (((1,), (1,)), ((), ())), preferred_element_type=F32) * jnp.exp(log_d - m_t)
        w_inter = jnp.exp(log_inter - m_t)
        num = (jnp.dot(s.astype(BF16), vb, preferred_element_type=F32)
               + w_inter * jnp.dot(qb, c_prev.astype(BF16), preferred_element_type=F32))
        den = jnp.sum(s, axis=1, keepdims=True) + w_inter * jnp.sum(q * n_prev, axis=1, keepdims=True)
        o_ref[:, sl] = num / jnp.maximum(jnp.abs(den), jnp.exp(-m_t))
        m_new = jnp.maximum(tot + m_prev, amax_all[:, h:h + 1])
        decay = jnp.exp(tot + m_prev - m_new)
        kw = k * jnp.exp(a_all[:, h:h + 1] - m_new)
        c_ref[st] = decay * c_prev + lax.dot_general(kw.astype(BF16), vb, (((0,), (0,)), ((), ())),
                                                     preferred_element_type=F32)
        n_ref[st] = decay * n_prev + jnp.sum(kw, axis=0, keepdims=True)
        m_ref[st] = m_new


def _mlstm_kernel(qf, kf, vf, gf, qb, kb, vb, gb, bias_ref, tri_ref, of_ref, ob_ref, c_ref, n_ref, m_ref,
                  *, n_chunks, seq_len):
    i = pl.program_id(0)

    @pl.when(i == 0)
    def _():
        c_ref[...] = jnp.zeros(c_ref.shape, F32)
        n_ref[...] = jnp.zeros(n_ref.shape, F32)
        m_ref[...] = jnp.zeros(m_ref.shape, F32)

    _mlstm_direction(0, i, qf, kf, vf, gf, bias_ref, tri_ref, of_ref, c_ref, n_ref, m_ref, seq_len)
    _mlstm_direction(1, n_chunks - 1 - i, qb, kb, vb, gb, bias_ref, tri_ref, ob_ref, c_ref, n_ref, m_ref, seq_len)


def mlstm_scan(qkvo, p_gate, gate_bias):
    seq_len = qkvo.shape[0]
    w = ML_WIDTH
    n_chunks = pl.cdiv(seq_len, ML_CHUNK)
    tri = np.tril(np.ones((ML_CHUNK, ML_CHUNK), np.float32))
    tri = jnp.asarray(np.stack([tri, tri.T]))
    fw = lambda col: (lambda i: (i, col))
    bw = lambda col: (lambda i: (n_chunks - 1 - i, col))
    blk = lambda im: pl.BlockSpec((ML_CHUNK, w), im)
    gblk = lambda im: pl.BlockSpec((ML_CHUNK, 4 * ML_HEADS), im)
    return pl.pallas_call(
        functools.partial(_mlstm_kernel, n_chunks=n_chunks, seq_len=seq_len),
        grid=(n_chunks,),
        in_specs=[blk(fw(0)), blk(fw(1)), blk(fw(2)), gblk(fw(0)),
                  blk(bw(0)), blk(bw(1)), blk(bw(2)), gblk(bw(0)),
                  pl.BlockSpec((1, 4 * ML_HEADS), lambda i: (0, 0)),
                  pl.BlockSpec((2, ML_CHUNK, ML_CHUNK), lambda i: (0, 0, 0))],
        out_specs=[blk(fw(0)), blk(bw(0))],
        out_shape=[jax.ShapeDtypeStruct((seq_len, w), F32), jax.ShapeDtypeStruct((seq_len, w), F32)],
        scratch_shapes=[pltpu.VMEM((2 * ML_HEADS, ML_HEAD_DIM, ML_HEAD_DIM), F32),
                        pltpu.VMEM((2 * ML_HEADS, 1, ML_HEAD_DIM), F32),
                        pltpu.VMEM((2 * ML_HEADS, 1, 1), F32)],
        compiler_params=_cparams(("arbitrary",)),
        name="mlstm_scan",
    )(qkvo, qkvo, qkvo, p_gate, qkvo, qkvo, qkvo, p_gate, gate_bias.reshape(1, -1), tri)


def _mlstm_finish_kernel(hf_ref, hb_ref, o_ref, g_ref, y_ref):
    for h in range(ML_HEADS):
        sl = slice(h * ML_HEAD_DIM, (h + 1) * ML_HEAD_DIM)
        x = hf_ref[:, sl] + hb_ref[:, sl]
        hn = x * lax.rsqrt(jnp.mean(x * x, axis=-1, keepdims=True) + RMS_EPS) * g_ref[:, sl]
        y_ref[:, sl] = (jax.nn.sigmoid(o_ref[:, sl]) * hn).astype(y_ref.dtype)


def mlstm_finish(h_fw, h_bw, qkvo, head_norm, tm, out_dtype):
    seq_len, w = h_fw.shape
    row = lambda i: (i, 0)
    return pl.pallas_call(
        _mlstm_finish_kernel,
        grid=(seq_len // tm,),
        in_specs=[pl.BlockSpec((tm, w), row), pl.BlockSpec((tm, w), row),
                  pl.BlockSpec((tm, w), lambda i: (i, 3)), pl.BlockSpec((1, w), lambda i: (0, 0))],
        out_specs=pl.BlockSpec((tm, w), row),
        out_shape=jax.ShapeDtypeStruct((seq_len, w), out_dtype),
        compiler_params=_cparams(("parallel",)),
        name="mlstm_finish",
    )(h_fw, h_bw, qkvo, head_norm.reshape(1, w))


TOK_TILE = 240
WIN = 256
I32 = jnp.int32
COMBINE_BUFS = 4
GATHER_EXPERTS = 2


def _router_kernel(h_ref, g_ref, w_ref, u_ref, a_ref):
    x = h_ref[...]
    u = (x * lax.rsqrt(jnp.mean(x * x, axis=-1, keepdims=True) + RMS_EPS) * g_ref[...]).astype(BF16)
    u_ref[...] = u
    logits = jnp.dot(u, w_ref[...].astype(BF16), preferred_element_type=F32)
    lane = lax.broadcasted_iota(I32, logits.shape, 1)
    logits = jnp.where(lane < N_EXPERTS, logits, -jnp.inf)
    z = jnp.exp(logits - jnp.max(logits, axis=-1, keepdims=True))
    a_ref[...] = z / jnp.sum(z, axis=-1, keepdims=True)


def router(h, g, w_router, tm):
    m, d = h.shape
    wp = jnp.pad(w_router, ((0, 0), (0, LANES - w_router.shape[1])))
    return pl.pallas_call(
        _router_kernel,
        grid=(m // tm,),
        in_specs=[pl.BlockSpec((tm, d), lambda i: (i, 0)), pl.BlockSpec((1, d), lambda i: (0, 0)),
                  pl.BlockSpec((d, LANES), lambda i: (0, 0))],
        out_specs=[pl.BlockSpec((tm, d), lambda i: (i, 0)), pl.BlockSpec((tm, LANES), lambda i: (i, 0))],
        out_shape=[jax.ShapeDtypeStruct((m, d), BF16), jax.ShapeDtypeStruct((m, LANES), F32)],
        compiler_params=_cparams(("parallel",)),
        name="router",
    )(h, g.reshape(1, d), wp)


def _topk_kernel(a_ref, tri_ref, eye_ref, posc_ref, posr_ref, start_ref, *, cap, n_tiles):
    bits = lax.bitcast_convert_type(a_ref[...], I32)

    def search(b, lo):
        cand = lo | (jnp.int32(1) << (30 - b))
        cnt = jnp.sum((bits >= cand).astype(I32), axis=0, keepdims=True)
        return jnp.where(cnt >= cap, cand, lo)

    thr = lax.fori_loop(0, 31, search, jnp.zeros((1, LANES), I32))
    n_gt = jnp.sum((bits > thr).astype(I32), axis=0, keepdims=True)
    need_eq = (cap - n_gt).astype(F32)
    tri = tri_ref[...]
    eye = eye_ref[...]

    def tile(t, carry):
        eq_before, sel_before = carry
        r0 = pl.multiple_of(t * TOK_TILE, 16)
        b = lax.bitcast_convert_type(a_ref[pl.ds(r0, TOK_TILE), :], I32)
        gt = b > thr
        eq = b == thr
        eq_rank = eq_before + jnp.dot(tri, eq.astype(BF16), preferred_element_type=F32)
        sel = gt | (eq & (eq_rank < need_eq))
        pos = sel_before + jnp.dot(tri, sel.astype(BF16), preferred_element_type=F32)
        posm = jnp.where(sel, pos, -1.0)
        posc_ref[pl.ds(r0, TOK_TILE), :] = posm
        posr_ref[t] = lax.dot_general(eye, posm, (((1,), (1,)), ((), ())), precision=HIGHEST,
                                      preferred_element_type=F32)
        start_ref[pl.ds(t, 1), :] = sel_before
        return (eq_before + jnp.sum(eq.astype(F32), axis=0, keepdims=True),
                sel_before + jnp.sum(sel.astype(F32), axis=0, keepdims=True))

    lax.fori_loop(0, n_tiles, tile, (jnp.zeros((1, LANES), F32), jnp.zeros((1, LANES), F32)))


def expert_choice_topk(aff_pad, cap):
    rows = aff_pad.shape[0]
    n_tiles = rows // TOK_TILE
    tri = jnp.asarray(np.tril(np.ones((TOK_TILE, TOK_TILE), np.float32), -1), BF16)
    eye = jnp.asarray(np.eye(LANES, dtype=np.float32))
    return pl.pallas_call(
        functools.partial(_topk_kernel, cap=cap, n_tiles=n_tiles),
        out_shape=[jax.ShapeDtypeStruct((rows, LANES), F32), jax.ShapeDtypeStruct((n_tiles, LANES, TOK_TILE), F32),
                   jax.ShapeDtypeStruct((n_tiles, LANES), F32)],
        compiler_params=pltpu.CompilerParams(vmem_limit_bytes=VMEM_LIMIT_BYTES),
        name="expert_choice_topk",
    )(aff_pad, tri, eye)


def _gather_kernel(start_ref, u_ref, pr_ref, o_ref, acc_ref, *, seq_len, cap_pad):
    t = pl.program_id(1)

    @pl.when(t == 0)
    def _():
        acc_ref[...] = jnp.zeros(acc_ref.shape, F32)

    row = t * TOK_TILE + lax.broadcasted_iota(I32, (TOK_TILE, 1), 0)
    u = jnp.where(row < seq_len, u_ref[...], 0)
    for i in range(GATHER_EXPERTS):
        e = pl.program_id(0) * GATHER_EXPERTS + i
        ws = pl.multiple_of((start_ref[e, t] // 16) * 16, 16)
        j = (ws + lax.broadcasted_iota(I32, (WIN, 1), 0)).astype(F32)
        onehot = (j == pr_ref[i, 0]).astype(BF16)
        acc_ref[i, pl.ds(ws, WIN), :] += jnp.dot(onehot, u, preferred_element_type=F32)

    @pl.when(t == pl.num_programs(1) - 1)
    def _():
        for i in range(GATHER_EXPERTS):
            o_ref[i] = acc_ref[i, 0:cap_pad, :].astype(o_ref.dtype)


def gather_tokens(u, posm_row, tile_start, cap_pad):
    seq_len, d = u.shape
    n_tiles = posm_row.shape[1]
    grid_spec = pltpu.PrefetchScalarGridSpec(
        num_scalar_prefetch=1,
        grid=(N_EXPERTS // GATHER_EXPERTS, n_tiles),
        in_specs=[pl.BlockSpec((TOK_TILE, d), lambda e, t, s: (t, 0)),
                  pl.BlockSpec((GATHER_EXPERTS, 1, 1, TOK_TILE), lambda e, t, s: (e, t, 0, 0))],
        out_specs=pl.BlockSpec((GATHER_EXPERTS, cap_pad, d), lambda e, t, s: (e, 0, 0)),
        scratch_shapes=[pltpu.VMEM((GATHER_EXPERTS, cap_pad + WIN, d), F32)],
    )
    return pl.pallas_call(
        functools.partial(_gather_kernel, seq_len=seq_len, cap_pad=cap_pad),
        grid_spec=grid_spec,
        out_shape=jax.ShapeDtypeStruct((N_EXPERTS, cap_pad, d), BF16),
        compiler_params=_cparams(("parallel", "arbitrary")),
        name="gather_tokens",
    )(tile_start, u, posm_row)


def _combine_kernel(start_ref, h_ref, pc_ref, a_ref, g_ref, ye_ref, o_ref, buf_ref, sem_ref, *, cap_pad):
    t = pl.program_id(0)
    total = N_EXPERTS * cap_pad

    def window_start(e):
        ws = e * cap_pad + (start_ref[e, t] // 16) * 16
        return pl.multiple_of(jnp.minimum(ws, total - WIN), 16)

    def window_copy(e, slot):
        return pltpu.make_async_copy(ye_ref.at[pl.ds(window_start(e), WIN), :], buf_ref.at[slot], sem_ref.at[slot])

    for e in range(COMBINE_BUFS - 1):
        window_copy(e, e).start()
    acc = h_ref[...]
    posm = pc_ref[...]
    aff = a_ref[...]
    col = lax.broadcasted_iota(I32, (1, WIN), 1).astype(F32)
    for e in range(N_EXPERTS):
        slot = e % COMBINE_BUFS
        ahead = e + COMBINE_BUFS - 1
        if ahead < N_EXPERTS:
            window_copy(ahead, ahead % COMBINE_BUFS).start()
        window_copy(e, slot).wait()
        rel = posm[:, e:e + 1] + (e * cap_pad - window_start(e)).astype(F32)
        onehot = (rel == col).astype(BF16)
        y = jnp.dot(onehot, buf_ref[slot], preferred_element_type=F32)
        acc = acc + aff[:, e:e + 1] * y
    o_ref[...] = acc * lax.rsqrt(jnp.mean(acc * acc, axis=-1, keepdims=True) + RMS_EPS) * g_ref[...]


def moe_combine(h, posm_col, aff_pad, ye, tile_start, g):
    seq_len, d = h.shape
    n_tiles = posm_col.shape[0] // TOK_TILE
    cap_pad = ye.shape[1]
    grid_spec = pltpu.PrefetchScalarGridSpec(
        num_scalar_prefetch=1,
        grid=(n_tiles,),
        in_specs=[pl.BlockSpec((TOK_TILE, d), lambda t, s: (t, 0)),
                  pl.BlockSpec((TOK_TILE, LANES), lambda t, s: (t, 0)),
                  pl.BlockSpec((TOK_TILE, LANES), lambda t, s: (t, 0)),
                  pl.BlockSpec((1, d), lambda t, s: (0, 0)),
                  pl.BlockSpec(memory_space=pl.ANY)],
        out_specs=pl.BlockSpec((TOK_TILE, d), lambda t, s: (t, 0)),
        scratch_shapes=[pltpu.VMEM((COMBINE_BUFS, WIN, d), BF16), pltpu.SemaphoreType.DMA((COMBINE_BUFS,))],
    )
    return pl.pallas_call(
        functools.partial(_combine_kernel, cap_pad=cap_pad),
        grid_spec=grid_spec,
        out_shape=jax.ShapeDtypeStruct((seq_len, d), F32),
        compiler_params=_cparams(("arbitrary",)),
        name="moe_combine",
    )(tile_start, h, posm_col, aff_pad, g.reshape(1, d), ye.reshape(N_EXPERTS * cap_pad, d))


def kernel(x, meta_tokens, norm_mix, w_in, ml_gate_bias, ml_head_norm, hy_conv_w, hy_filt_w1, hy_filt_b1,
           hy_filt_w2, hy_filt_b2, hy_filt_freq, hy_filt_w3, hy_bias, w_branch_a, w_branch_b, w_out,
           norm_ffn, w_router, w_gate, w_up, w_down, norm_final):
    b_ = x.shape[0]
    assert b_ == 1
    h = jnp.concatenate([meta_tokens, x[0]], axis=0)
    seq_len = h.shape[0]
    tm = 912
    assert seq_len % tm == 0
    layer = 0

    w_in_t = jnp.transpose(w_in[layer])
    qkvo = norm_proj(h, norm_mix[layer], w_in_t, OFF_G, 0, tm, 1024)
    w_hm = w_in_t[OFF_HY:].astype(BF16)
    p_hm = norm_proj(h, norm_mix[layer], w_hm, w_hm.shape[0], 0, tm, 1024)
    w_g = jnp.pad(w_in_t[OFF_G:OFF_HY], ((0, 128 - 4 * ML_HEADS), (0, 0)))
    p_g = norm_proj(h, norm_mix[layer], w_g, 128, 0, tm, 128)[:, :4 * ML_HEADS]

    h_fw, h_bw = mlstm_scan(qkvo, p_g, ml_gate_bias[layer])
    y_a = mlstm_finish(h_fw, h_bw, qkvo, ml_head_norm[layer], tm, BF16)
    max_decay = math.log(HY_DECAY_TARGET) / HY_FAST_DECAY_PCT
    min_decay = math.log(HY_DECAY_TARGET) / HY_SLOW_DECAY_PCT
    deltas = jnp.abs(jnp.linspace(min_decay, max_decay, HY_WIDTH, dtype=F32))
    y_b = hyena_branch(p_hm, 0, hy_conv_w[layer], hy_filt_w1[layer], hy_filt_b1[layer], hy_filt_w2[layer],
                       hy_filt_b2[layer], hy_filt_freq[layer], hy_filt_w3[layer], hy_bias[layer], deltas,
                       seq_len, HY_BANDS)

    merged = gated_dual_mm(y_a, y_b, p_hm, 3 * HY_WIDTH, w_branch_a[layer], w_branch_b[layer], tm, 512)
    h = mm_residual(merged, w_out[layer], h, tm, 1024)

    cap = EC_CAPACITY * seq_len // N_EXPERTS
    cap_pad = -(-cap // 16) * 16
    n_tiles = -(-seq_len // TOK_TILE)
    u, aff = router(h, norm_ffn[layer], w_router[layer], tm)
    aff_pad = jnp.pad(aff, ((0, n_tiles * TOK_TILE - seq_len), (0, 0)), constant_values=-1.0)
    posm_col, posm_row, start = expert_choice_topk(aff_pad, cap)
    tile_start = start[:, :N_EXPERTS].T.astype(jnp.int32)
    posm_row = jnp.transpose(posm_row[:, :N_EXPERTS, :], (1, 0, 2))[:, :, None, :]
    xe = gather_tokens(u, posm_row, tile_start, cap_pad)
    ye = expert_ffn(xe, w_gate[layer], w_up[layer], w_down[layer], 256)
    out = moe_combine(h, posm_col, aff_pad, ye, tile_start, norm_final)
    return out[None, N_META:]
```
